```python
import jax
import jax.numpy as jnp
from jax import lax
import numpy as np

D_MODEL = 1024
BATCH = 16
SEQ = 2048
DEPTH = 4

N_MIXERS = 4
N_REPEAT = DEPTH // N_MIXERS
RMS_EPS = 1e-6
ROPE_THETA = 10000.0

DSA_HEADS = 16
DSA_KV_HEADS = 4
DSA_HEAD_DIM = 64
DSA_IDX_HEADS = 8
DSA_IDX_DIM = 128
DSA_TOPK = 256
DSA_QBLOCK = 128
DSA_Q = DSA_HEADS * DSA_HEAD_DIM
DSA_KV = DSA_KV_HEADS * DSA_HEAD_DIM
DSA_SPLITS = (DSA_Q, DSA_Q + DSA_KV, DSA_Q + 2 * DSA_KV, 2 * DSA_Q + 2 * DSA_KV,
              2 * DSA_Q + 2 * DSA_KV + DSA_IDX_HEADS * DSA_IDX_DIM,
              2 * DSA_Q + 2 * DSA_KV + DSA_IDX_HEADS * DSA_IDX_DIM + DSA_IDX_HEADS)
DSA_IN = DSA_SPLITS[-1] + DSA_IDX_DIM
DSA_IDX_SCALE = (DSA_IDX_HEADS * DSA_IDX_DIM) ** -0.5

LRU_WIDTH = D_MODEL
LRU_BLOCKS = 16
LRU_BLOCK_DIM = LRU_WIDTH // LRU_BLOCKS
LRU_CONV = 4
LRU_C = 8.0

RWKV_HEAD_DIM = 64
RWKV_HEADS = D_MODEL // RWKV_HEAD_DIM
RWKV_DECAY_LORA = 64
RWKV_AAA_LORA = 64
RWKV_GN_EPS = 64e-5

GLA_HEADS = 4
GLA_KEY_DIM = D_MODEL // 2
GLA_VAL_DIM = D_MODEL
GLA_DK = GLA_KEY_DIM // GLA_HEADS
GLA_DV = GLA_VAL_DIM // GLA_HEADS
GLA_GATE_RANK = 16
GLA_GATE_NORM = 16.0
GLA_CHUNK = 64
GLA_SPLITS = (GLA_KEY_DIM, 2 * GLA_KEY_DIM, 2 * GLA_KEY_DIM + GLA_VAL_DIM,
              2 * GLA_KEY_DIM + 2 * GLA_VAL_DIM)
GLA_IN = GLA_SPLITS[-1] + GLA_GATE_RANK

kernel_name = 'hybrid_dsa_rglru_rwkv7_gla_trunk'


def rms_norm(x, gain, eps=RMS_EPS):
    xf = x.astype(jnp.float32)
    y = xf * lax.rsqrt(jnp.mean(xf * xf, axis=-1, keepdims=True) + eps)
    return (y * gain.astype(jnp.float32)).astype(x.dtype)


def rope(x):
    seq, d = x.shape[1], x.shape[-1]
    half = d // 2
    inv_freq = ROPE_THETA ** (-jnp.arange(half, dtype=jnp.float32) / half)
    ang = jnp.arange(seq, dtype=jnp.float32)[:, None] * inv_freq[None, :]
    cos = jnp.cos(ang)[None, :, None, :]
    sin = jnp.sin(ang)[None, :, None, :]
    xf = x.astype(jnp.float32)
    x1, x2 = xf[..., :half], xf[..., half:]
    return jnp.concatenate([x1 * cos - x2 * sin, x2 * cos + x1 * sin], axis=-1).astype(x.dtype)


def dsa_mixer(h, w_in, q_gain, k_gain, w_out):
    f32 = jnp.float32
    bsz, seq, _ = h.shape
    q, k, v, g, qi, wi, ki = jnp.split(h @ w_in, DSA_SPLITS, axis=-1)
    q = rope(rms_norm(q.reshape(bsz, seq, DSA_HEADS, DSA_HEAD_DIM), q_gain))
    k = rope(rms_norm(k.reshape(bsz, seq, DSA_KV_HEADS, DSA_HEAD_DIM), k_gain))
    v = v.reshape(bsz, seq, DSA_KV_HEADS, DSA_HEAD_DIM)
    qi = rope(qi.reshape(bsz, seq, DSA_IDX_HEADS, DSA_IDX_DIM)).astype(f32)
    ki = rope(ki.reshape(bsz, seq, 1, DSA_IDX_DIM))[:, :, 0].astype(f32)
    wi = wi.astype(f32) * DSA_IDX_SCALE
    n_sel = min(DSA_TOPK, seq // 4)
    n_blk = seq // DSA_QBLOCK
    rep = DSA_HEADS // DSA_KV_HEADS
    scale = DSA_HEAD_DIM ** -0.5
    starts = jnp.arange(n_blk, dtype=jnp.int32) * DSA_QBLOCK
    key_pos = jnp.arange(seq, dtype=jnp.int32)

    def one_sequence(args):
        q_s, k_s, v_s, qi_s, wi_s, ki_s = args

        def one_block(blk):
            q_b, qi_b, wi_b, t0 = blk
            q_pos = t0 + jnp.arange(DSA_QBLOCK, dtype=jnp.int32)
            idx_logits = jnp.einsum('qhd,sd->qhs', qi_b, ki_s)
            score = jnp.einsum('qh,qhs->qs', wi_b, jax.nn.relu(idx_logits))
            score = jnp.where(key_pos[None, :] <= q_pos[:, None], score, -jnp.inf)
            _, sel = lax.top_k(score, n_sel)
            valid = sel <= q_pos[:, None]
            k_sel = k_s[sel]
            v_sel = v_s[sel]
            q_g = q_b.reshape(DSA_QBLOCK, DSA_KV_HEADS, rep, DSA_HEAD_DIM)
            s = jnp.einsum('qgrd,qkgd->qgrk', q_g, k_sel).astype(f32) * scale
            s = jnp.where(valid[:, None, None, :], s, -jnp.inf)
            p = jax.nn.softmax(s, axis=-1).astype(v_sel.dtype)
            o = jnp.einsum('qgrk,qkgd->qgrd', p, v_sel)
            return o.reshape(DSA_QBLOCK, DSA_Q)

        blocks = (q_s.reshape(n_blk, DSA_QBLOCK, DSA_HEADS, DSA_HEAD_DIM),
                  qi_s.reshape(n_blk, DSA_QBLOCK, DSA_IDX_HEADS, DSA_IDX_DIM),
                  wi_s.reshape(n_blk, DSA_QBLOCK, DSA_IDX_HEADS), starts)
        return lax.map(one_block, blocks).reshape(seq, DSA_Q)

    o = lax.map(one_sequence, (q, k, v, qi, wi, ki))
    return (o * jax.nn.silu(g)) @ w_out


def causal_depthwise_conv(u, w, b):
    width = w.shape[0]
    out = lax.conv_general_dilated(u, w[:, None, :].astype(u.dtype), window_strides=(1,),
                                   padding=[(width - 1, 0)],
                                   dimension_numbers=('NWC', 'WIO', 'NWC'),
                                   feature_group_count=u.shape[-1])
    return out + b


def rglru_mixer(h, w_in, conv_w, conv_b, gate_a_w, gate_a_b, gate_x_w, gate_x_b, lam, w_out):
    f32 = jnp.float32
    bsz, seq, _ = h.shape
    u, g = jnp.split(h @ w_in, 2, axis=-1)
    u = causal_depthwise_conv(u, conv_w, conv_b)
    u_blk = u.reshape(bsz, seq, LRU_BLOCKS, LRU_BLOCK_DIM)
    gr = jnp.einsum('bsnc,ncd->bsnd', u_blk, gate_a_w).reshape(bsz, seq, LRU_WIDTH) + gate_a_b
    gi = jnp.einsum('bsnc,ncd->bsnd', u_blk, gate_x_w).reshape(bsz, seq, LRU_WIDTH) + gate_x_b
    r = jax.nn.sigmoid(gr.astype(f32))
    i = jax.nn.sigmoid(gi.astype(f32))
    log_a = -LRU_C * r * jax.nn.softplus(-lam.astype(f32))
    a = jnp.exp(log_a)
    b = jnp.sqrt(-jnp.expm1(2.0 * log_a)) * (i * u.astype(f32))

    def combine(left, right):
        a_l, b_l = left
        a_r, b_r = right
        return a_l * a_r, a_r * b_l + b_r

    _, hs = lax.associative_scan(combine, (a, b), axis=1)
    return (hs.astype(h.dtype) * jax.nn.silu(g)) @ w_out


def rwkv7_mixer(h, mu, w_in, w0, w1, w2, a0, a1, a2, k_k, k_a, r_k, ln_w, ln_b, w_out):
    f32 = jnp.float32
    bsz, seq, dm = h.shape
    hshape = (RWKV_HEADS, RWKV_HEAD_DIM)
    h_prev = jnp.pad(h, ((0, 0), (1, 0), (0, 0)))[:, :-1]
    xs = h[None] + (h_prev - h)[None] * mu[:, None, None, :]
    r, k, v, g = jnp.einsum('nbsd,nde->nbse', xs[:4], w_in)
    w_log = -jax.nn.softplus(-(w0 + jnp.tanh(xs[4] @ w1) @ w2).astype(f32)) - 0.5
    decay = jnp.exp(-jnp.exp(w_log))
    a = jax.nn.sigmoid((a0 + (xs[5] @ a1) @ a2).astype(f32))

    def heads(t):
        return t.astype(f32).reshape(bsz, seq, *hshape)

    r, k, v, decay, a = heads(r), heads(k), heads(v), heads(decay), heads(a)
    kk = k * k_k.reshape(hshape)
    kk = kk / jnp.maximum(jnp.sqrt(jnp.sum(kk * kk, axis=-1, keepdims=True)), 1e-12)
    k = k * (1.0 + (a - 1.0) * k_a.reshape(hshape))

    def step(state, inp):
        r_t, w_t, k_t, v_t, a_t, b_t = inp
        sa = jnp.einsum('bhij,bhj->bhi', state, a_t)
        state = (state * w_t[:, :, None, :] + sa[..., None] * b_t[:, :, None, :]
                 + v_t[..., None] * k_t[:, :, None, :])
        return state, jnp.einsum('bhij,bhj->bhi', state, r_t)

    s0 = jnp.zeros((bsz, RWKV_HEADS, RWKV_HEAD_DIM, RWKV_HEAD_DIM), f32)
    seq_major = tuple(jnp.moveaxis(t, 1, 0) for t in (r, decay, k, v, -kk, kk * a))
    _, y = lax.scan(step, s0, seq_major)
    y = jnp.moveaxis(y, 0, 1)
    mean = jnp.mean(y, axis=-1, keepdims=True)
    var = jnp.mean(jnp.square(y - mean), axis=-1, keepdims=True)
    y = (y - mean) * lax.rsqrt(var + RWKV_GN_EPS) * ln_w.reshape(hshape) + ln_b.reshape(hshape)
    y = y + jnp.sum(r * k * r_k, axis=-1, keepdims=True) * v
    y = y.reshape(bsz, seq, dm).astype(h.dtype) * jax.nn.silu(g)
    return y @ w_out


def gla_mixer(h, w_in, alpha_w2, alpha_b, norm_gain, w_out):
    f32 = jnp.float32
    bsz, seq, _ = h.shape
    q, k, v, g, a_low = jnp.split(h @ w_in, GLA_SPLITS, axis=-1)
    log_alpha = jax.nn.log_sigmoid((a_low @ alpha_w2 + alpha_b).astype(f32)) / GLA_GATE_NORM
    n_chunk = seq // GLA_CHUNK

    def chunks(t, d):
        return t.astype(f32).reshape(bsz, n_chunk, GLA_CHUNK, GLA_HEADS, d)

    q = chunks(q, GLA_DK) * GLA_DK ** -0.5
    k = chunks(k, GLA_DK)
    v = chunks(v, GLA_DV)
    cum = lax.cumsum(chunks(log_alpha, GLA_DK), axis=2)
    last = cum[:, :, -1]
    q_dec = q * jnp.exp(cum)
    k_inv = k * jnp.exp(-cum)
    k_end = k * jnp.exp(last[:, :, None] - cum)
    causal = jnp.tril(jnp.ones((GLA_CHUNK, GLA_CHUNK), dtype=bool))
    att = jnp.where(causal, jnp.einsum('bnihd,bnjhd->bnhij', q_dec, k_inv), 0.0)
    o_intra = jnp.einsum('bnhij,bnjhe->bnihe', att, v)
    chunk_kv = jnp.einsum('bnjhd,bnjhe->bnhde', k_end, v)

    def carry_state(state, inp):
        kv_c, last_c = inp
        return state * jnp.exp(last_c)[..., None] + kv_c, state

    s0 = jnp.zeros((bsz, GLA_HEADS, GLA_DK, GLA_DV), f32)
    _, s_prev = lax.scan(carry_state, s0, (jnp.moveaxis(chunk_kv, 1, 0), jnp.moveaxis(last, 1, 0)))
    o_inter = jnp.einsum('bnihd,nbhde->bnihe', q_dec, s_prev)
    o = (o_intra + o_inter).reshape(bsz, seq, GLA_HEADS, GLA_DV)
    o = rms_norm(o, norm_gain).reshape(bsz, seq, GLA_VAL_DIM).astype(h.dtype)
    return (o * jax.nn.silu(g)) @ w_out


def setup_inputs(seed: int = 0) -> dict:
    key = jax.random.key(seed)
    keys = iter(jax.random.split(key, 64))
    f32 = jnp.float32
    D, R = D_MODEL, N_REPEAT

    def nrm(shape, scale):
        return jax.random.normal(next(keys), shape, f32) * scale

    def unif(shape, lo, hi):
        return jax.random.uniform(next(keys), shape, f32, lo, hi)

    lru_u = unif((R, LRU_WIDTH), 0.9, 0.999)
    lru_base = lru_u ** (1.0 / LRU_C)
    return {
        'x': nrm((BATCH, SEQ, D), 1.0),
        'c': nrm((BATCH, D), 1.0),
        'ln_gain': 1.0 + nrm((DEPTH, D), 0.02),
        'mod_w': nrm((DEPTH, D, 3 * D), 0.5 * D ** -0.5),
        'mod_b': nrm((DEPTH, 3 * D), 0.02),
        'dsa_w_in': nrm((R, D, DSA_IN), D ** -0.5),
        'dsa_q_gain': 1.0 + nrm((R, DSA_HEAD_DIM), 0.02),
        'dsa_k_gain': 1.0 + nrm((R, DSA_HEAD_DIM), 0.02),
        'dsa_w_out': nrm((R, DSA_Q, D), DSA_Q ** -0.5),
        'lru_w_in': nrm((R, D, 2 * LRU_WIDTH), D ** -0.5),
        'lru_conv_w': nrm((R, LRU_CONV, LRU_WIDTH), LRU_CONV ** -0.5),
        'lru_conv_b': nrm((R, LRU_WIDTH), 0.02),
        'lru_gate_a_w': nrm((R, LRU_BLOCKS, LRU_BLOCK_DIM, LRU_BLOCK_DIM), LRU_BLOCK_DIM ** -0.5),
        'lru_gate_a_b': nrm((R, LRU_WIDTH), 0.02),
        'lru_gate_x_w': nrm((R, LRU_BLOCKS, LRU_BLOCK_DIM, LRU_BLOCK_DIM), LRU_BLOCK_DIM ** -0.5),
        'lru_gate_x_b': nrm((R, LRU_WIDTH), 0.02),
        'lru_lambda': jnp.log(lru_base) - jnp.log1p(-lru_base),
        'lru_w_out': nrm((R, LRU_WIDTH, D), LRU_WIDTH ** -0.5),
        'rwkv_mu': unif((R, 6, D), 0.0, 1.0),
        'rwkv_w_in': nrm((R, 4, D, D), D ** -0.5),
        'rwkv_w0': unif((R, D), -6.0, -1.0),
        'rwkv_w1': nrm((R, D, RWKV_DECAY_LORA), D ** -0.5),
        'rwkv_w2': nrm((R, RWKV_DECAY_LORA, D), 0.1 * RWKV_DECAY_LORA ** -0.5),
        'rwkv_a0': nrm((R, D), 0.1),
        'rwkv_a1': nrm((R, D, RWKV_AAA_LORA), D ** -0.5),
        'rwkv_a2': nrm((R, RWKV_AAA_LORA, D), 0.1 * RWKV_AAA_LORA ** -0.5),
        'rwkv_k_k': 0.85 + nrm((R, D), 0.02),
        'rwkv_k_a': 1.0 + nrm((R, D), 0.02),
        'rwkv_r_k': nrm((R, RWKV_HEADS, RWKV_HEAD_DIM), 0.1),
        'rwkv_ln_w': 1.0 + nrm((R, D), 0.02),
        'rwkv_ln_b': nrm((R, D), 0.02),
        'rwkv_w_out': nrm((R, D, D), D ** -0.5),
        'gla_w_in': nrm((R, D, GLA_IN), D ** -0.5),
        'gla_alpha_w2': nrm((R, GLA_GATE_RANK, GLA_KEY_DIM), GLA_GATE_RANK ** -0.5),
        'gla_alpha_b': nrm((R, GLA_KEY_DIM), 0.1),
        'gla_norm_gain': 1.0 + nrm((R, GLA_DV), 0.02),
        'gla_w_out': nrm((R, GLA_VAL_DIM, D), GLA_VAL_DIM ** -0.5),
    }


def reference(x, c, ln_gain, mod_w, mod_b,
              dsa_w_in, dsa_q_gain, dsa_k_gain, dsa_w_out,
              lru_w_in, lru_conv_w, lru_conv_b, lru_gate_a_w, lru_gate_a_b,
              lru_gate_x_w, lru_gate_x_b, lru_lambda, lru_w_out,
              rwkv_mu, rwkv_w_in, rwkv_w0, rwkv_w1, rwkv_w2, rwkv_a0, rwkv_a1, rwkv_a2,
              rwkv_k_k, rwkv_k_a, rwkv_r_k, rwkv_ln_w, rwkv_ln_b, rwkv_w_out,
              gla_w_in, gla_alpha_w2, gla_alpha_b, gla_norm_gain, gla_w_out):
    c_act = jax.nn.silu(c)
    for layer in range(DEPTH):
        mixer, r = layer % N_MIXERS, layer // N_MIXERS
        mod = c_act @ mod_w[layer] + mod_b[layer]
        shift, scale, gate = jnp.split(mod, 3, axis=-1)
        h = rms_norm(x, ln_gain[layer]) * (1.0 + scale[:, None, :]) + shift[:, None, :]
        if mixer == 0:
            y = dsa_mixer(h, dsa_w_in[r], dsa_q_gain[r], dsa_k_gain[r], dsa_w_out[r])
        elif mixer == 1:
            y = rglru_mixer(h, lru_w_in[r], lru_conv_w[r], lru_conv_b[r], lru_gate_a_w[r],
                            lru_gate_a_b[r], lru_gate_x_w[r], lru_gate_x_b[r], lru_lambda[r],
                            lru_w_out[r])
        elif mixer == 2:
            y = rwkv7_mixer(h, rwkv_mu[r], rwkv_w_in[r], rwkv_w0[r], rwkv_w1[r], rwkv_w2[r],
                            rwkv_a0[r], rwkv_a1[r], rwkv_a2[r], rwkv_k_k[r], rwkv_k_a[r],
                            rwkv_r_k[r], rwkv_ln_w[r], rwkv_ln_b[r], rwkv_w_out[r])
        else:
            y = gla_mixer(h, gla_w_in[r], gla_alpha_w2[r], gla_alpha_b[r], gla_norm_gain[r],
                          gla_w_out[r])
        x = x + gate[:, None, :] * y
    return x
```

```python
import functools

import jax
import jax.numpy as jnp
from jax import lax
from jax.experimental import pallas as pl
from jax.experimental.pallas import tpu as pltpu

F32 = jnp.float32
BF16 = jnp.bfloat16
HIGHEST = lax.Precision.HIGHEST

LANES = 128
SUBLANES = 8
VMEM_LIMIT_BYTES = 52 * 1024 * 1024

RMS_EPS = 1e-6
ROPE_THETA = 10000.0

DSA_HEADS = 16
DSA_KV_HEADS = 4
DSA_HEAD_DIM = 64
DSA_IDX_HEADS = 8
DSA_IDX_DIM = 128
DSA_TOPK = 256
DSA_QBLOCK = 128
DSA_IDX_SCALE = (DSA_IDX_HEADS * DSA_IDX_DIM) ** -0.5

LRU_BLOCKS = 16
LRU_CONV = 4
LRU_C = 8.0
LRU_GROUP = 256

RWKV_HEAD_DIM = 64
RWKV_GN_EPS = 64e-5
RWKV_CHUNK = 64

GLA_HEADS = 4
GLA_GATE_RANK = 16
GLA_GATE_NORM = 16.0
GLA_CHUNK = 64

NEG_BIG = -1e30
NT_DIMS = (((1,), (1,)), ((), ()))


def _params(*semantics):
    return pltpu.CompilerParams(dimension_semantics=semantics,
                                vmem_limit_bytes=VMEM_LIMIT_BYTES)


def _dot(a, b):
    return jnp.dot(a.astype(BF16), b.astype(BF16), preferred_element_type=F32)


def _dot_nt(a, b):
    return lax.dot_general(a.astype(BF16), b.astype(BF16), NT_DIMS,
                           preferred_element_type=F32)


def _split3(x):
    hi = x.astype(BF16)
    r1 = x - hi.astype(F32)
    mid = r1.astype(BF16)
    lo = (r1 - mid.astype(F32)).astype(BF16)
    return hi, mid, lo


def _dot_exact_lhs(m01, x):
    hi, mid, lo = _split3(x)
    return (jnp.dot(m01, hi, preferred_element_type=F32)
            + jnp.dot(m01, mid, preferred_element_type=F32)
            + jnp.dot(m01, lo, preferred_element_type=F32))


def _dot_exact_rhs(x, m01):
    hi, mid, lo = _split3(x)
    return (jnp.dot(hi, m01, preferred_element_type=F32)
            + jnp.dot(mid, m01, preferred_element_type=F32)
            + jnp.dot(lo, m01, preferred_element_type=F32))


def _group_sum(z, g01):
    cols = [_dot_exact_rhs(z[:, c * LANES:(c + 1) * LANES], g01)
            for c in range(z.shape[1] // LANES)]
    return cols[0] if len(cols) == 1 else jnp.concatenate(cols, axis=1)


def _silu(x):
    return x * jax.nn.sigmoid(x)


def _softplus(z):
    return jnp.maximum(z, 0.0) + jnp.log1p(jnp.exp(-jnp.abs(z)))


def _prenorm(x, gain, scale, shift):
    ms = jnp.mean(x * x, axis=-1, keepdims=True)
    y = x * lax.rsqrt(ms + RMS_EPS) * gain
    return y * (1.0 + scale) + shift


def _mod_kernel(c_ref, w_ref, b_ref, o_ref):
    o_ref[0] = jnp.dot(_silu(c_ref[...]), w_ref[0], precision=HIGHEST,
                       preferred_element_type=F32) + b_ref[0]


def _modulation(c, mod_w, mod_b):
    depth, d, _ = mod_w.shape
    bsz = c.shape[0]
    return pl.pallas_call(
        _mod_kernel,
        out_shape=jax.ShapeDtypeStruct((depth, bsz, 3 * d), F32),
        grid=(depth, 3),
        in_specs=[pl.BlockSpec((bsz, d), lambda l, j: (0, 0)),
                  pl.BlockSpec((1, d, d), lambda l, j: (l, 0, j)),
                  pl.BlockSpec((1, 1, d), lambda l, j: (l, 0, j))],
        out_specs=pl.BlockSpec((1, bsz, d), lambda l, j: (l, 0, j)),
        compiler_params=_params("arbitrary", "arbitrary"),
        name="adaln_mod",
    )(c, mod_w, mod_b.reshape(depth, 1, 3 * d))


def _mod_specs(d):
    return [pl.BlockSpec((1, 1, d), lambda b, t, j=j: (b, 0, j)) for j in range(3)]


def _proj_kernel(widths, x_ref, gain_ref, shift_ref, scale_ref, w_ref, *out_refs):
    h = _prenorm(x_ref[0], gain_ref[...], scale_ref[0], shift_ref[0]).astype(BF16)
    off = 0
    for o_ref, n in zip(out_refs, widths):
        o_ref[0] = jnp.dot(h, w_ref[:, off:off + n], preferred_element_type=F32)
        off += n


def _norm_proj(x, gain, mod3, w_bf16, widths, tm=256):
    bsz, seq, d = x.shape
    shift_spec, scale_spec, _ = _mod_specs(d)
    return pl.pallas_call(
        functools.partial(_proj_kernel, widths),
        out_shape=[jax.ShapeDtypeStruct((bsz, seq, n), F32) for n in widths],
        grid=(bsz, seq // tm),
        in_specs=[pl.BlockSpec((1, tm, d), lambda b, t: (b, t, 0)),
                  pl.BlockSpec((1, d), lambda b, t: (0, 0)),
                  shift_spec, scale_spec,
                  pl.BlockSpec(w_bf16.shape, lambda b, t: (0, 0))],
        out_specs=[pl.BlockSpec((1, tm, n), lambda b, t: (b, t, 0)) for n in widths],
        compiler_params=_params("parallel", "parallel"),
        name="norm_proj",
    )(x, gain.reshape(1, d), mod3, mod3, w_bf16)


def _outproj_kernel(y_ref, g_ref, x_ref, gate_ref, w_ref, o_ref):
    a = (y_ref[0] * _silu(g_ref[0])).astype(BF16)
    o_ref[0] = x_ref[0] + gate_ref[0] * jnp.dot(a, w_ref[...], preferred_element_type=F32)


def _gated_outproj(y, g, x, mod3, w_bf16, tm=256):
    bsz, seq, d = x.shape
    n = y.shape[-1]
    gate_spec = _mod_specs(d)[2]
    return pl.pallas_call(
        _outproj_kernel,
        out_shape=jax.ShapeDtypeStruct((bsz, seq, d), F32),
        grid=(bsz, seq // tm),
        in_specs=[pl.BlockSpec((1, tm, n), lambda b, t: (b, t, 0)),
                  pl.BlockSpec((1, tm, n), lambda b, t: (b, t, 0)),
                  pl.BlockSpec((1, tm, d), lambda b, t: (b, t, 0)),
                  gate_spec,
                  pl.BlockSpec((n, d), lambda b, t: (0, 0))],
        out_specs=pl.BlockSpec((1, tm, d), lambda b, t: (b, t, 0)),
        compiler_params=_params("parallel", "parallel"),
        name="gated_outproj",
    )(y, g, x, mod3, w_bf16)


def _head_group_ones():
    r = jnp.arange(LANES) // DSA_HEAD_DIM
    return (r[:, None] == r[None, :]).astype(BF16)


def _rope_tables(seq, dim, reps):
    half = dim // 2
    inv_freq = ROPE_THETA ** (-jnp.arange(half, dtype=F32) / half)
    ang = jnp.arange(seq, dtype=F32)[:, None] * inv_freq[None, :]
    cos = jnp.concatenate([jnp.cos(ang), jnp.cos(ang)], axis=1)
    sin = jnp.concatenate([-jnp.sin(ang), jnp.sin(ang)], axis=1)
    return jnp.tile(cos, (1, reps)), jnp.tile(sin, (1, reps))


def _rope64(x, cos, sin, lane_lo):
    cols = []
    for c in range(x.shape[1] // LANES):
        xb = x[:, c * LANES:(c + 1) * LANES]
        rot = jnp.where(lane_lo, pltpu.roll(xb, 96, 1), pltpu.roll(xb, 32, 1))
        cols.append(xb * cos + rot * sin)
    return cols[0] if len(cols) == 1 else jnp.concatenate(cols, axis=1)


def _rope128(x, cos, sin):
    cols = []
    for c in range(x.shape[1] // LANES):
        xb = x[:, c * LANES:(c + 1) * LANES]
        cols.append(xb * cos + pltpu.roll(xb, 64, 1) * sin)
    return cols[0] if len(cols) == 1 else jnp.concatenate(cols, axis=1)


_DSA_Q = DSA_HEADS * DSA_HEAD_DIM
_DSA_KV2 = DSA_KV_HEADS * LANES
_DSA_QI = DSA_IDX_HEADS * DSA_IDX_DIM
_DSA_OFF_Q = 0
_DSA_OFF_G = _DSA_OFF_Q + _DSA_Q
_DSA_OFF_QI = _DSA_OFF_G + _DSA_Q
_DSA_OFF_K = _DSA_OFF_QI + _DSA_QI
_DSA_OFF_V = _DSA_OFF_K + _DSA_KV2
_DSA_OFF_KI = _DSA_OFF_V + _DSA_KV2
_DSA_OFF_WI = _DSA_OFF_KI + DSA_IDX_DIM
_DSA_COLS = _DSA_OFF_WI + LANES


def _dsa_proj_kernel(x_ref, gain_ref, shift_ref, scale_ref, w_ref, qgain_ref, kgain_ref,
                     cos64_ref, sin64_ref, cos128_ref, sin128_ref, g01_ref,
                     q_ref, g_ref, qi_ref, kt_ref, v_ref, ki_ref, wi_ref):
    h = _prenorm(x_ref[0], gain_ref[...], scale_ref[0], shift_ref[0]).astype(BF16)
    g01 = g01_ref[...]
    cos64, sin64 = cos64_ref[...], sin64_ref[...]
    cos128, sin128 = cos128_ref[...], sin128_ref[...]
    lane_lo = (lax.broadcasted_iota(jnp.int32, cos64.shape, 1) % DSA_HEAD_DIM) < DSA_HEAD_DIM // 2

    def head_norm_rope(raw, gain):
        ms = _group_sum(raw * raw, g01) * (1.0 / DSA_HEAD_DIM)
        return _rope64(raw * lax.rsqrt(ms + RMS_EPS) * gain, cos64, sin64, lane_lo)

    def col(off, n):
        return jnp.dot(h, w_ref[:, off:off + n], preferred_element_type=F32)

    q = head_norm_rope(col(_DSA_OFF_Q, _DSA_Q), qgain_ref[...])
    q_ref[0] = (q * DSA_HEAD_DIM ** -0.5).astype(BF16)
    g_ref[0] = col(_DSA_OFF_G, _DSA_Q)
    qi_ref[0] = _rope128(col(_DSA_OFF_QI, _DSA_QI), cos128, sin128).astype(BF16)
    k2 = head_norm_rope(col(_DSA_OFF_K, _DSA_KV2), kgain_ref[...])
    kt = k2.T.astype(BF16)
    for kv in range(DSA_KV_HEADS):
        kt_ref[0, kv] = kt[kv * LANES:(kv + 1) * LANES, :]
    v2 = col(_DSA_OFF_V, _DSA_KV2).astype(BF16)
    for kv in range(DSA_KV_HEADS):
        v_ref[0, kv] = v2[:, kv * LANES:(kv + 1) * LANES]
    ki_ref[0] = _rope128(col(_DSA_OFF_KI, DSA_IDX_DIM), cos128, sin128).astype(BF16)
    wi_ref[0] = col(_DSA_OFF_WI, LANES) * DSA_IDX_SCALE


def _dsa_weights(w_in):
    d = w_in.shape[0]
    q_end = _DSA_Q
    k_end = q_end + DSA_KV_HEADS * DSA_HEAD_DIM
    v_end = k_end + DSA_KV_HEADS * DSA_HEAD_DIM
    g_end = v_end + _DSA_Q
    qi_end = g_end + _DSA_QI
    wi_end = qi_end + DSA_IDX_HEADS
    wq, wk, wv, wg = w_in[:, :q_end], w_in[:, q_end:k_end], w_in[:, k_end:v_end], w_in[:, v_end:g_end]
    wqi, wwi, wki = w_in[:, g_end:qi_end], w_in[:, qi_end:wi_end], w_in[:, wi_end:]

    def dup(w):
        w = w.reshape(d, DSA_KV_HEADS, 1, DSA_HEAD_DIM)
        return jnp.broadcast_to(w, (d, DSA_KV_HEADS, 2, DSA_HEAD_DIM)).reshape(d, _DSA_KV2)

    wwi = jnp.pad(wwi, ((0, 0), (0, LANES - DSA_IDX_HEADS)))
    return jnp.concatenate([wq, wg, wqi, dup(wk), dup(wv), wki, wwi], axis=1).astype(BF16)


def _dsa_project(x, gain, mod3, w_in, q_gain, k_gain, tm=256):
    bsz, seq, d = x.shape
    w = _dsa_weights(w_in)
    cos64, sin64 = _rope_tables(seq, DSA_HEAD_DIM, LANES // DSA_HEAD_DIM)
    cos128, sin128 = _rope_tables(seq, DSA_IDX_DIM, 1)
    shift_spec, scale_spec, _ = _mod_specs(d)
    full = lambda shape: pl.BlockSpec(shape, lambda b, t: (0,) * len(shape))
    table = pl.BlockSpec((tm, LANES), lambda b, t: (t, 0))
    row = lambda n, dt: jax.ShapeDtypeStruct((bsz, seq, n), dt)
    return pl.pallas_call(
        _dsa_proj_kernel,
        out_shape=[row(_DSA_Q, BF16), row(_DSA_Q, F32), row(_DSA_QI, BF16),
                   jax.ShapeDtypeStruct((bsz, DSA_KV_HEADS, LANES, seq), BF16),
                   jax.ShapeDtypeStruct((bsz, DSA_KV_HEADS, seq, LANES), BF16),
                   row(DSA_IDX_DIM, BF16), row(LANES, F32)],
        grid=(bsz, seq // tm),
        in_specs=[pl.BlockSpec((1, tm, d), lambda b, t: (b, t, 0)),
                  full((1, d)), shift_spec, scale_spec, full(w.shape),
                  full((1, _DSA_Q)), full((1, _DSA_KV2)),
                  table, table, table, table, full((LANES, LANES))],
        out_specs=[pl.BlockSpec((1, tm, _DSA_Q), lambda b, t: (b, t, 0)),
                   pl.BlockSpec((1, tm, _DSA_Q), lambda b, t: (b, t, 0)),
                   pl.BlockSpec((1, tm, _DSA_QI), lambda b, t: (b, t, 0)),
                   pl.BlockSpec((1, DSA_KV_HEADS, LANES, tm), lambda b, t: (b, 0, 0, t)),
                   pl.BlockSpec((1, DSA_KV_HEADS, tm, LANES), lambda b, t: (b, 0, t, 0)),
                   pl.BlockSpec((1, tm, DSA_IDX_DIM), lambda b, t: (b, t, 0)),
                   pl.BlockSpec((1, tm, LANES), lambda b, t: (b, t, 0))],
        compiler_params=_params("parallel", "parallel"),
        name="dsa_proj",
    )(x, gain.reshape(1, d), mod3, mod3, w,
      jnp.tile(q_gain, DSA_HEADS).reshape(1, _DSA_Q),
      jnp.tile(k_gain, _DSA_KV2 // DSA_HEAD_DIM).reshape(1, _DSA_KV2),
      cos64, sin64, cos128, sin128, _head_group_ones())


def _sortable_to_float(key):
    bits = jnp.where(key >= 0, key, key ^ jnp.int32(0x7FFFFFFF))
    return lax.bitcast_convert_type(bits, F32)


def _dsa_attn_kernel(n_sel, qi_ref, wi_ref, ki_ref, q_ref, kt_ref, v_ref, tri_ref, o_ref,
                     score_ref, bias_ref):
    blk = pl.program_id(1)
    qb, seq = score_ref.shape
    ki = ki_ref[0]
    wi = wi_ref[0]

    acc = None
    for hh in range(DSA_IDX_HEADS):
        logits = lax.dot_general(qi_ref[0, :, hh * DSA_IDX_DIM:(hh + 1) * DSA_IDX_DIM], ki, NT_DIMS,
                                 preferred_element_type=F32)
        term = wi[:, hh:hh + 1] * jnp.maximum(logits, 0.0)
        acc = term if acc is None else acc + term
    q_pos = blk * qb + lax.broadcasted_iota(jnp.int32, (qb, 1), 0)
    key_pos = lax.broadcasted_iota(jnp.int32, (1, seq), 1)
    score_ref[...] = jnp.where(key_pos <= q_pos, acc, -jnp.inf)

    k_sel = jnp.float32(n_sel)

    def count_ge(thr):
        return jnp.sum(jnp.where(score_ref[...] >= thr, 1.0, 0.0), axis=1, keepdims=True)

    int_min = jnp.int32(-2 ** 31)
    key0 = jnp.where(count_ge(jnp.zeros((qb, 1), F32)) >= k_sel, jnp.int32(0), int_min)

    def bit_step(b, key):
        cand = key + lax.shift_left(jnp.int32(1), jnp.int32(30) - b)
        return jnp.where(count_ge(_sortable_to_float(cand)) >= k_sel, cand, key)

    thr = _sortable_to_float(lax.fori_loop(0, 31, bit_step, key0))

    score = score_ref[...]
    gt = score > thr
    need = k_sel - jnp.sum(jnp.where(gt, 1.0, 0.0), axis=1, keepdims=True)
    take_all = q_pos < n_sel
    tri = tri_ref[...]
    run = jnp.zeros((qb, 1), F32)
    for c in range(seq // LANES):
        sl = slice(c * LANES, (c + 1) * LANES)
        eq = jnp.where(score[:, sl] == thr, 1.0, 0.0)
        incl = jnp.dot(eq.astype(BF16), tri, preferred_element_type=F32)
        tie_ok = (incl - eq + run) < need
        run = run + incl[:, LANES - 1:LANES]
        sel = gt[:, sl] | ((eq > 0.0) & tie_ok) | take_all
        causal = key_pos[:, sl] <= q_pos
        bias_ref[:, sl] = jnp.where(sel & causal, 0.0, NEG_BIG)

    lane = lax.broadcasted_iota(jnp.int32, (qb, LANES), 1)
    lo = lane < DSA_HEAD_DIM
    bias = bias_ref[...]
    pairs_per_kv = DSA_HEADS // DSA_KV_HEADS // 2
    for kv in range(DSA_KV_HEADS):
        kt = kt_ref[0, kv]
        vv = v_ref[0, kv]
        for c in range(pairs_per_kv):
            pair = kv * pairs_per_kv + c
            qp = q_ref[0, :, pair * LANES:(pair + 1) * LANES]
            halves = []
            for first in (True, False):
                qm = jnp.where(lo if first else ~lo, qp, jnp.zeros_like(qp))
                s = jnp.dot(qm, kt, preferred_element_type=F32) + bias
                p = jnp.exp(s - jnp.max(s, axis=1, keepdims=True))
                denom = jnp.sum(p, axis=1, keepdims=True)
                halves.append(jnp.dot(p.astype(BF16), vv, preferred_element_type=F32) / denom)
            o_ref[0, :, pair * LANES:(pair + 1) * LANES] = jnp.where(lo, halves[0], halves[1])


def _dsa_attention(q, qi, kt2, v2, ki, wi):
    bsz, seq, _ = q.shape
    qb = DSA_QBLOCK
    n_sel = min(DSA_TOPK, seq // 4)
    tri = (jnp.arange(LANES)[:, None] <= jnp.arange(LANES)[None, :]).astype(BF16)
    return pl.pallas_call(
        functools.partial(_dsa_attn_kernel, n_sel),
        out_shape=jax.ShapeDtypeStruct((bsz, seq, _DSA_Q), F32),
        grid=(bsz, seq // qb),
        in_specs=[pl.BlockSpec((1, qb, _DSA_QI), lambda b, i: (b, i, 0)),
                  pl.BlockSpec((1, qb, LANES), lambda b, i: (b, i, 0)),
                  pl.BlockSpec((1, seq, DSA_IDX_DIM), lambda b, i: (b, 0, 0)),
                  pl.BlockSpec((1, qb, _DSA_Q), lambda b, i: (b, i, 0)),
                  pl.BlockSpec((1, DSA_KV_HEADS, LANES, seq), lambda b, i: (b, 0, 0, 0)),
                  pl.BlockSpec((1, DSA_KV_HEADS, seq, LANES), lambda b, i: (b, 0, 0, 0)),
                  pl.BlockSpec((LANES, LANES), lambda b, i: (0, 0))],
        out_specs=pl.BlockSpec((1, qb, _DSA_Q), lambda b, i: (b, i, 0)),
        scratch_shapes=[pltpu.VMEM((qb, seq), F32), pltpu.VMEM((qb, seq), F32)],
        compiler_params=_params("parallel", "parallel"),
        name="dsa_attn",
    )(qi, wi, ki, q, kt2, v2, tri)


def _dsa_layer(x, gain, mod3, w_in, q_gain, k_gain, w_out):
    q, g, qi, kt2, v2, ki, wi = _dsa_project(x, gain, mod3, w_in, q_gain, k_gain)
    o = _dsa_attention(q, qi, kt2, v2, ki, wi)
    return _gated_outproj(o, g, x, mod3, w_out.astype(BF16))


def _lru_kernel(u_ref, cw_ref, cb_ref, wa_ref, ba_ref, wx_ref, bx_ref, lam_ref, o_ref,
                ubuf, a_buf, b_buf, h_carry):
    tt = u_ref.shape[1]
    width = u_ref.shape[2]
    halo = SUBLANES

    @pl.when(pl.program_id(1) == 0)
    def _():
        ubuf[0:halo, :] = jnp.zeros((halo, width), F32)
        h_carry[...] = jnp.zeros_like(h_carry)

    ubuf[halo:halo + tt, :] = u_ref[0]
    cw = cw_ref[...]
    u = cb_ref[...]
    for j in range(LRU_CONV):
        start = halo - (LRU_CONV - 1) + j
        u = u + cw[j:j + 1, :] * ubuf[start:start + tt, :]
    ubuf[0:halo, :] = ubuf[tt:tt + halo, :]

    sp = _softplus(-lam_ref[...])
    for c in range(width // LRU_GROUP):
        sl = slice(c * LRU_GROUP, (c + 1) * LRU_GROUP)
        uc = u[:, sl]
        ub = uc.astype(BF16)
        r = jax.nn.sigmoid(jnp.dot(ub, wa_ref[c], preferred_element_type=F32) + ba_ref[:, sl])
        i = jax.nn.sigmoid(jnp.dot(ub, wx_ref[c], preferred_element_type=F32) + bx_ref[:, sl])
        a = jnp.exp(-LRU_C * r * sp[:, sl])
        a_buf[:, sl] = a
        b_buf[:, sl] = jnp.sqrt(1.0 - a * a) * (i * uc)

    row = lax.broadcasted_iota(jnp.int32, (SUBLANES, width), 0)

    def group(gi, h_prev):
        r0 = pl.multiple_of(gi * SUBLANES, SUBLANES)
        a = a_buf[pl.ds(r0, SUBLANES), :]
        b = b_buf[pl.ds(r0, SUBLANES), :]
        for s in (1, 2, 4):
            ok = row >= s
            b = jnp.where(ok, a * pltpu.roll(b, s, 0) + b, b)
            a = jnp.where(ok, a * pltpu.roll(a, s, 0), a)
        h = a * h_prev + b
        o_ref[0, pl.ds(r0, SUBLANES), :] = h
        return jnp.broadcast_to(h[SUBLANES - 1:SUBLANES, :], (SUBLANES, width))

    h_carry[...] = lax.fori_loop(0, tt // SUBLANES, group, h_carry[...])


def _lru_gate_blocks(w):
    nb, bd, _ = w.shape
    per = LRU_GROUP // bd
    w = w.reshape(nb // per, per, bd, bd)
    eye = jnp.eye(per, dtype=w.dtype)
    return jnp.einsum('gpcd,pq->gpcqd', w, eye).reshape(nb // per, LRU_GROUP, LRU_GROUP).astype(BF16)


def _lru_scan(u_raw, conv_w, conv_b, gate_a_w, gate_a_b, gate_x_w, gate_x_b, lam, tt=256):
    bsz, seq, width = u_raw.shape
    ngroups = width // LRU_GROUP
    vec = lambda: pl.BlockSpec((1, width), lambda b, t: (0, 0))
    gatew = lambda: pl.BlockSpec((ngroups, LRU_GROUP, LRU_GROUP), lambda b, t: (0, 0, 0))
    cw = jnp.pad(conv_w, ((0, SUBLANES - LRU_CONV), (0, 0)))
    return pl.pallas_call(
        _lru_kernel,
        out_shape=jax.ShapeDtypeStruct((bsz, seq, width), F32),
        grid=(bsz, seq // tt),
        in_specs=[pl.BlockSpec((1, tt, width), lambda b, t: (b, t, 0)),
                  pl.BlockSpec((SUBLANES, width), lambda b, t: (0, 0)),
                  vec(), gatew(), vec(), gatew(), vec(), vec()],
        out_specs=pl.BlockSpec((1, tt, width), lambda b, t: (b, t, 0)),
        scratch_shapes=[pltpu.VMEM((tt + SUBLANES, width), F32),
                        pltpu.VMEM((tt, width), F32), pltpu.VMEM((tt, width), F32),
                        pltpu.VMEM((SUBLANES, width), F32)],
        compiler_params=_params("parallel", "arbitrary"),
        name="rglru_scan",
    )(u_raw, cw, conv_b.reshape(1, width), _lru_gate_blocks(gate_a_w), gate_a_b.reshape(1, width),
      _lru_gate_blocks(gate_x_w), gate_x_b.reshape(1, width), lam.reshape(1, width))


def _lru_layer(x, gain, mod3, w_in, conv_w, conv_b, gate_a_w, gate_a_b, gate_x_w, gate_x_b, lam, w_out):
    width = w_in.shape[1] // 2
    u_raw, g = _norm_proj(x, gain, mod3, w_in.astype(BF16), (width, width))
    hs = _lru_scan(u_raw, conv_w, conv_b, gate_a_w, gate_a_b, gate_x_w, gate_x_b, lam)
    return _gated_outproj(hs, g, x, mod3, w_out.astype(BF16))


def _rwkv_proj_kernel(x_ref, gain_ref, shift_ref, scale_ref, mu_ref, w_ref, w1_ref, w2_ref,
                      a1_ref, a2_ref, w0_ref, a0_ref, kk_ref, ka_ref, rk_ref, g01_ref,
                      r_out, lw_out, k_out, v_out, kk_out, a_out, g_out, bonus_out, carry):
    tm = x_ref.shape[1]
    d = x_ref.shape[2]

    @pl.when(pl.program_id(1) == 0)
    def _():
        carry[...] = jnp.zeros_like(carry)

    h = _prenorm(x_ref[0], gain_ref[...], scale_ref[0], shift_ref[0])
    first = lax.broadcasted_iota(jnp.int32, (tm, d), 0) == 0
    h_prev = jnp.where(first, carry[0:1, :], pltpu.roll(h, 1, 0))
    carry[...] = jnp.broadcast_to(h[tm - 1:tm, :], carry.shape)
    delta = h_prev - h
    mu = mu_ref[...]

    def mix(n):
        return (h + delta * mu[n:n + 1, :]).astype(BF16)

    r = jnp.dot(mix(0), w_ref[0], preferred_element_type=F32)
    k = jnp.dot(mix(1), w_ref[1], preferred_element_type=F32)
    v = jnp.dot(mix(2), w_ref[2], preferred_element_type=F32)
    g_out[0] = jnp.dot(mix(3), w_ref[3], preferred_element_type=F32)
    w_lora = _dot(jnp.tanh(jnp.dot(mix(4), w1_ref[...], preferred_element_type=F32)), w2_ref[...])
    w_log = -_softplus(-(w0_ref[...] + w_lora)) - 0.5
    lw_out[0] = -jnp.exp(w_log)
    a_lora = _dot(jnp.dot(mix(5), a1_ref[...], preferred_element_type=F32), a2_ref[...])
    a = jax.nn.sigmoid(a0_ref[...] + a_lora)
    g01 = g01_ref[...]
    kk = k * kk_ref[...]
    kk = kk / jnp.maximum(jnp.sqrt(_group_sum(kk * kk, g01)), 1e-12)
    k = k * (1.0 + (a - 1.0) * ka_ref[...])
    r_out[0] = r
    k_out[0] = k
    v_out[0] = v
    kk_out[0] = kk
    a_out[0] = a
    bonus_out[0] = _group_sum(r * k * rk_ref[...], g01) * v


def _rwkv_project(x, gain, mod3, mu, w_in, w0, w1, w2, a0, a1, a2, k_k, k_a, r_k, tm=256):
    bsz, seq, d = x.shape
    shift_spec, scale_spec, _ = _mod_specs(d)
    full = lambda shape: pl.BlockSpec(shape, lambda b, t: (0,) * len(shape))
    rank = w1.shape[1]
    pad_c = lambda w: jnp.pad(w, ((0, 0), (0, LANES - rank))).astype(BF16)
    pad_r = lambda w: jnp.pad(w, ((0, LANES - rank), (0, 0))).astype(BF16)
    vec = lambda p: p.reshape(1, d)
    out = jax.ShapeDtypeStruct((bsz, seq, d), F32)
    tile = pl.BlockSpec((1, tm, d), lambda b, t: (b, t, 0))
    return pl.pallas_call(
        _rwkv_proj_kernel,
        out_shape=[out] * 8,
        grid=(bsz, seq // tm),
        in_specs=[tile, full((1, d)), shift_spec, scale_spec, full((SUBLANES, d)),
                  full((4, d, d)), full((d, LANES)), full((LANES, d)), full((d, LANES)),
                  full((LANES, d)), full((1, d)), full((1, d)), full((1, d)), full((1, d)),
                  full((1, d)), full((LANES, LANES))],
        out_specs=[tile] * 8,
        scratch_shapes=[pltpu.VMEM((SUBLANES, d), F32)],
        compiler_params=_params("parallel", "arbitrary"),
        name="rwkv_proj",
    )(x, gain.reshape(1, d), mod3, mod3, jnp.pad(mu, ((0, SUBLANES - mu.shape[0]), (0, 0))),
      w_in.astype(BF16), pad_c(w1), pad_r(w2), pad_c(a1), pad_r(a2), vec(w0), vec(a0),
      vec(k_k), vec(k_a), vec(r_k), _head_group_ones())


def _rwkv_chunk_kernel(r_ref, lw_ref, k_ref, v_ref, kk_ref, a_ref, bonus_ref, lnw_ref, lnb_ref,
                       g01_ref, ltri_ref, o_ref, state):
    tc = r_ref.shape[1]
    d = r_ref.shape[2]
    cs = RWKV_CHUNK
    hd = RWKV_HEAD_DIM

    @pl.when(pl.program_id(1) == 0)
    def _():
        state[...] = jnp.zeros_like(state)

    lane = lax.broadcasted_iota(jnp.int32, (cs, LANES), 1)
    lo = lane < hd
    ri = lax.broadcasted_iota(jnp.int32, (LANES, LANES), 0)
    ci = lax.broadcasted_iota(jnp.int32, (LANES, LANES), 1)
    same = (ri < cs) == (ci < cs)
    strict = same & (ci < ri)
    incl = same & (ci <= ri)
    eye = jnp.where(ri == ci, 1.0, 0.0)
    g01 = g01_ref[...]
    ltri = ltri_ref[...]

    def stack(z):
        return jnp.concatenate([z, z], axis=0)

    def split_heads(z):
        return jnp.concatenate([jnp.where(lo, z, 0.0), jnp.where(lo, 0.0, z)], axis=0)

    def own(z):
        return jnp.where(lo, z[0:cs], z[cs:2 * cs])

    for c in range(tc // cs):
        rows = slice(c * cs, (c + 1) * cs)
        g_all = _dot_exact_lhs(ltri, lw_ref[0, rows, :])
        for p in range(d // LANES):
            sl = slice(p * LANES, (p + 1) * LANES)
            g_inc = g_all[:, sl]
            g_exc = g_inc - lw_ref[0, rows, sl]
            g_last = g_inc[cs - 1:cs, :]
            kk = kk_ref[0, rows, sl]
            kp = k_ref[0, rows, sl]
            vp = v_ref[0, rows, sl]
            bvec = kk * a_ref[0, rows, sl]
            e_neg = jnp.exp(-g_inc)
            e_end = jnp.exp(g_last - g_inc)
            at = -kk * jnp.exp(g_exc)
            rt = r_ref[0, rows, sl] * jnp.exp(g_inc)
            bt = split_heads(bvec * e_neg)
            kt = split_heads(kp * e_neg)
            la = stack(at)
            lr = stack(rt)
            n_ab = jnp.where(strict, _dot_nt(la, bt), 0.0)
            n_ak = jnp.where(strict, _dot_nt(la, kt), 0.0)
            n_rb = jnp.where(incl, _dot_nt(lr, bt), 0.0)
            n_rk = jnp.where(incl, _dot_nt(lr, kt), 0.0)
            inv = eye + n_ab
            pw = n_ab
            for _ in range(5):
                pw = _dot(pw, pw)
                inv = inv + _dot(inv, pw)
            s_bd = state[p]
            vv = stack(vp)
            rhs = _dot_nt(at, s_bd) + own(_dot(n_ak, vv))
            u = own(_dot(inv, stack(rhs)))
            y = _dot_nt(rt, s_bd) + own(_dot(n_rb, stack(u))) + own(_dot(n_rk, vv))
            upd = _dot(jnp.concatenate([u, vp], axis=0).T,
                       jnp.concatenate([bvec * e_end, kp * e_end], axis=0))
            state[p] = jnp.where(same, s_bd * jnp.exp(g_last) + upd, 0.0)
            mean = _group_sum(y, g01) * (1.0 / hd)
            yc = y - mean
            var = _group_sum(yc * yc, g01) * (1.0 / hd)
            o_ref[0, rows, sl] = (yc * lax.rsqrt(var + RWKV_GN_EPS) * lnw_ref[:, sl] + lnb_ref[:, sl]
                                  + bonus_ref[0, rows, sl])


def _rwkv_recurrence(r, lw, k, v, kk, a, bonus, ln_w, ln_b, tc=128):
    bsz, seq, d = r.shape
    tile = pl.BlockSpec((1, tc, d), lambda b, t: (b, t, 0))
    vec = pl.BlockSpec((1, d), lambda b, t: (0, 0))
    ltri = (jnp.arange(RWKV_CHUNK)[:, None] >= jnp.arange(RWKV_CHUNK)[None, :]).astype(BF16)
    return pl.pallas_call(
        _rwkv_chunk_kernel,
        out_shape=jax.ShapeDtypeStruct((bsz, seq, d), F32),
        grid=(bsz, seq // tc),
        in_specs=[tile] * 7 + [vec, vec,
                               pl.BlockSpec((LANES, LANES), lambda b, t: (0, 0)),
                               pl.BlockSpec((RWKV_CHUNK, RWKV_CHUNK), lambda b, t: (0, 0))],
        out_specs=tile,
        scratch_shapes=[pltpu.VMEM((d // LANES, LANES, LANES), F32)],
        compiler_params=_params("parallel", "arbitrary"),
        name="rwkv_chunk",
    )(r, lw, k, v, kk, a, bonus, ln_w.reshape(1, d), ln_b.reshape(1, d), _head_group_ones(), ltri)


def _rwkv_layer(x, gain, mod3, mu, w_in, w0, w1, w2, a0, a1, a2, k_k, k_a, r_k, ln_w, ln_b, w_out):
    r, lw, k, v, kk, a, g, bonus = _rwkv_project(x, gain, mod3, mu, w_in, w0, w1, w2, a0, a1, a2,
                                                 k_k, k_a, r_k)
    y = _rwkv_recurrence(r, lw, k, v, kk, a, bonus, ln_w, ln_b)
    return _gated_outproj(y, g, x, mod3, w_out.astype(BF16))


def _gla_kernel(q_ref, k_ref, v_ref, alow_ref, w2_ref, ab_ref, gain_ref, ltri_ref, o_ref, state):
    tt = q_ref.shape[1]
    key_dim = q_ref.shape[2]
    dk = key_dim // GLA_HEADS
    dv = v_ref.shape[2] // GLA_HEADS
    cs = GLA_CHUNK

    @pl.when(pl.program_id(1) == 0)
    def _():
        state[...] = jnp.zeros_like(state)

    ri = lax.broadcasted_iota(jnp.int32, (cs, cs), 0)
    ci = lax.broadcasted_iota(jnp.int32, (cs, cs), 1)
    causal = ci <= ri
    ltri = ltri_ref[...]
    z = _dot(alow_ref[0], w2_ref[...]) + ab_ref[...]
    log_alpha = -_softplus(-z) * (1.0 / GLA_GATE_NORM)

    for c in range(tt // cs):
        rows = slice(c * cs, (c + 1) * cs)
        cum = _dot_exact_lhs(ltri, log_alpha[rows, :])
        last = cum[cs - 1:cs, :]
        kc = k_ref[0, rows, :]
        q_dec = q_ref[0, rows, :] * (dk ** -0.5) * jnp.exp(cum)
        k_inv = kc * jnp.exp(-cum)
        k_end = kc * jnp.exp(last - cum)
        decay = jnp.exp(last)
        for hh in range(GLA_HEADS):
            ks = slice(hh * dk, (hh + 1) * dk)
            vs = slice(hh * dv, (hh + 1) * dv)
            vh = v_ref[0, rows, vs]
            st = state[hh]
            att = jnp.where(causal, _dot_nt(q_dec[:, ks], k_inv[:, ks]), 0.0)
            o = _dot(att, vh) + _dot_nt(q_dec[:, ks], st)
            state[hh] = st * decay[:, ks] + _dot(vh.T, k_end[:, ks])
            ms = jnp.mean(o * o, axis=-1, keepdims=True)
            o_ref[0, rows, vs] = o * lax.rsqrt(ms + RMS_EPS) * gain_ref[...]


def _gla_core(q, k, v, a_low, alpha_w2, alpha_b, norm_gain, tt=256):
    bsz, seq, key_dim = q.shape
    val_dim = v.shape[2]
    dk, dv = key_dim // GLA_HEADS, val_dim // GLA_HEADS
    w2 = jnp.pad(alpha_w2, ((0, LANES - GLA_GATE_RANK), (0, 0))).astype(BF16)
    ltri = (jnp.arange(GLA_CHUNK)[:, None] >= jnp.arange(GLA_CHUNK)[None, :]).astype(BF16)
    full = lambda shape: pl.BlockSpec(shape, lambda b, t: (0,) * len(shape))
    return pl.pallas_call(
        _gla_kernel,
        out_shape=jax.ShapeDtypeStruct((bsz, seq, val_dim), F32),
        grid=(bsz, seq // tt),
        in_specs=[pl.BlockSpec((1, tt, key_dim), lambda b, t: (b, t, 0)),
                  pl.BlockSpec((1, tt, key_dim), lambda b, t: (b, t, 0)),
                  pl.BlockSpec((1, tt, val_dim), lambda b, t: (b, t, 0)),
                  pl.BlockSpec((1, tt, LANES), lambda b, t: (b, t, 0)),
                  full((LANES, key_dim)), full((1, key_dim)), full((1, dv)),
                  full((GLA_CHUNK, GLA_CHUNK))],
        out_specs=pl.BlockSpec((1, tt, val_dim), lambda b, t: (b, t, 0)),
        scratch_shapes=[pltpu.VMEM((GLA_HEADS, dv, dk), F32)],
        compiler_params=_params("parallel", "arbitrary"),
        name="gla_chunk",
    )(q, k, v, a_low, w2, alpha_b.reshape(1, key_dim), norm_gain.reshape(1, dv), ltri)


def _gla_layer(x, gain, mod3, w_in, alpha_w2, alpha_b, norm_gain, w_out):
    d = x.shape[2]
    key_dim, val_dim = alpha_w2.shape[1], w_out.shape[0]
    w = jnp.pad(w_in, ((0, 0), (0, LANES - GLA_GATE_RANK))).astype(BF16)
    q, k, v, g, a_low = _norm_proj(x, gain, mod3, w, (key_dim, key_dim, val_dim, val_dim, LANES))
    o = _gla_core(q, k, v, a_low, alpha_w2, alpha_b, norm_gain)
    return _gated_outproj(o, g, x, mod3, w_out.astype(BF16))


def kernel(x, c, ln_gain, mod_w, mod_b, dsa_w_in, dsa_q_gain, dsa_k_gain, dsa_w_out, lru_w_in, lru_conv_w, lru_conv_b, lru_gate_a_w, lru_gate_a_b, lru_gate_x_w, lru_gate_x_b, lru_lambda, lru_w_out, rwkv_mu, rwkv_w_in, rwkv_w0, rwkv_w1, rwkv_w2, rwkv_a0, rwkv_a1, rwkv_a2, rwkv_k_k, rwkv_k_a, rwkv_r_k, rwkv_ln_w, rwkv_ln_b, rwkv_w_out, gla_w_in, gla_alpha_w2, gla_alpha_b, gla_norm_gain, gla_w_out):
    depth = mod_w.shape[0]
    bsz, _, d = x.shape
    mod = _modulation(c, mod_w, mod_b)
    for layer in range(depth):
        mixer, r = layer % 4, layer // 4
        mod3 = mod[layer].reshape(bsz, 1, 3 * d)
        gain = ln_gain[layer]
        if mixer == 0:
            x = _dsa_layer(x, gain, mod3, dsa_w_in[r], dsa_q_gain[r], dsa_k_gain[r], dsa_w_out[r])
        elif mixer == 1:
            x = _lru_layer(x, gain, mod3, lru_w_in[r], lru_conv_w[r], lru_conv_b[r], lru_gate_a_w[r],
                           lru_gate_a_b[r], lru_gate_x_w[r], lru_gate_x_b[r], lru_lambda[r], lru_w_out[r])
        elif mixer == 2:
            x = _rwkv_layer(x, gain, mod3, rwkv_mu[r], rwkv_w_in[r], rwkv_w0[r], rwkv_w1[r], rwkv_w2[r],
                            rwkv_a0[r], rwkv_a1[r], rwkv_a2[r], rwkv_k_k[r], rwkv_k_a[r],
                            rwkv_r_k[r].reshape(-1), rwkv_ln_w[r], rwkv_ln_b[r], rwkv_w_out[r])
        else:
            x = _gla_layer(x, gain, mod3, gla_w_in[r], gla_alpha_w2[r], gla_alpha_b[r],
                           gla_norm_gain[r], gla_w_out[r])
    return x
```

```python
import functools

import jax
import jax.numpy as jnp
from jax import lax
from jax.experimental import pallas as pl
from jax.experimental.pallas import tpu as pltpu

F32 = jnp.float32
BF16 = jnp.bfloat16
HIGHEST = lax.Precision.HIGHEST

LANES = 128
SUBLANES = 8
VMEM_LIMIT_BYTES = 52 * 1024 * 1024

RMS_EPS = 1e-6
ROPE_THETA = 10000.0

DSA_HEADS = 16
DSA_KV_HEADS = 4
DSA_HEAD_DIM = 64
DSA_IDX_HEADS = 8
DSA_IDX_DIM = 128
DSA_TOPK = 256
DSA_QBLOCK = 128
DSA_KEY_CHUNK = 512
LOG2E = 1.4426950408889634
DSA_IDX_SCALE = (DSA_IDX_HEADS * DSA_IDX_DIM) ** -0.5

LRU_BLOCKS = 16
LRU_CONV = 4
LRU_C = 8.0
LRU_GROUP = 256

RWKV_HEAD_DIM = 64
RWKV_GN_EPS = 64e-5
RWKV_CHUNK = 64

GLA_HEADS = 4
GLA_GATE_RANK = 16
GLA_GATE_NORM = 16.0
GLA_CHUNK = 64

NEG_BIG = -1e30
NT_DIMS = (((1,), (1,)), ((), ()))


def _params(*semantics):
    return pltpu.CompilerParams(dimension_semantics=semantics,
                                vmem_limit_bytes=VMEM_LIMIT_BYTES)


def _dot(a, b):
    return jnp.dot(a.astype(BF16), b.astype(BF16), preferred_element_type=F32)


def _dot_nt(a, b):
    return lax.dot_general(a.astype(BF16), b.astype(BF16), NT_DIMS,
                           preferred_element_type=F32)


def _split3(x):
    hi = x.astype(BF16)
    r1 = x - hi.astype(F32)
    mid = r1.astype(BF16)
    lo = (r1 - mid.astype(F32)).astype(BF16)
    return hi, mid, lo


def _dot_exact_lhs(m01, x):
    hi, mid, lo = _split3(x)
    return (jnp.dot(m01, hi, preferred_element_type=F32)
            + jnp.dot(m01, mid, preferred_element_type=F32)
            + jnp.dot(m01, lo, preferred_element_type=F32))


def _dot_exact_rhs(x, m01):
    hi, mid, lo = _split3(x)
    return (jnp.dot(hi, m01, preferred_element_type=F32)
            + jnp.dot(mid, m01, preferred_element_type=F32)
            + jnp.dot(lo, m01, preferred_element_type=F32))


def _group_sum(z, g01):
    cols = [_dot_exact_rhs(z[:, c * LANES:(c + 1) * LANES], g01)
            for c in range(z.shape[1] // LANES)]
    return cols[0] if len(cols) == 1 else jnp.concatenate(cols, axis=1)


def _silu(x):
    return x * jax.nn.sigmoid(x)


def _softplus(z):
    return jnp.maximum(z, 0.0) + jnp.log1p(jnp.exp(-jnp.abs(z)))


def _prenorm(x, gain, scale, shift):
    ms = jnp.mean(x * x, axis=-1, keepdims=True)
    y = x * lax.rsqrt(ms + RMS_EPS) * gain
    return y * (1.0 + scale) + shift


def _mod_kernel(c_ref, w_ref, b_ref, o_ref):
    o_ref[0] = jnp.dot(_silu(c_ref[...]), w_ref[0], precision=HIGHEST,
                       preferred_element_type=F32) + b_ref[0]


def _modulation(c, mod_w, mod_b):
    depth, d, _ = mod_w.shape
    bsz = c.shape[0]
    return pl.pallas_call(
        _mod_kernel,
        out_shape=jax.ShapeDtypeStruct((depth, bsz, 3 * d), F32),
        grid=(depth, 3),
        in_specs=[pl.BlockSpec((bsz, d), lambda l, j: (0, 0)),
                  pl.BlockSpec((1, d, d), lambda l, j: (l, 0, j)),
                  pl.BlockSpec((1, 1, d), lambda l, j: (l, 0, j))],
        out_specs=pl.BlockSpec((1, bsz, d), lambda l, j: (l, 0, j)),
        compiler_params=_params("arbitrary", "arbitrary"),
        name="adaln_mod",
    )(c, mod_w, mod_b.reshape(depth, 1, 3 * d))


def _mod_specs(d):
    return [pl.BlockSpec((1, 1, d), lambda b, t, j=j: (b, 0, j)) for j in range(3)]


def _proj_kernel(widths, x_ref, gain_ref, shift_ref, scale_ref, w_ref, *out_refs):
    h = _prenorm(x_ref[0], gain_ref[...], scale_ref[0], shift_ref[0]).astype(BF16)
    off = 0
    for o_ref, n in zip(out_refs, widths):
        o_ref[0] = jnp.dot(h, w_ref[:, off:off + n], preferred_element_type=F32)
        off += n


def _norm_proj(x, gain, mod3, w_bf16, widths, tm=256):
    bsz, seq, d = x.shape
    shift_spec, scale_spec, _ = _mod_specs(d)
    return pl.pallas_call(
        functools.partial(_proj_kernel, widths),
        out_shape=[jax.ShapeDtypeStruct((bsz, seq, n), F32) for n in widths],
        grid=(bsz, seq // tm),
        in_specs=[pl.BlockSpec((1, tm, d), lambda b, t: (b, t, 0)),
                  pl.BlockSpec((1, d), lambda b, t: (0, 0)),
                  shift_spec, scale_spec,
                  pl.BlockSpec(w_bf16.shape, lambda b, t: (0, 0))],
        out_specs=[pl.BlockSpec((1, tm, n), lambda b, t: (b, t, 0)) for n in widths],
        compiler_params=_params("parallel", "parallel"),
        name="norm_proj",
    )(x, gain.reshape(1, d), mod3, mod3, w_bf16)


def _outproj_kernel(y_ref, g_ref, x_ref, gate_ref, w_ref, o_ref):
    a = (y_ref[0] * _silu(g_ref[0])).astype(BF16)
    o_ref[0] = x_ref[0] + gate_ref[0] * jnp.dot(a, w_ref[...], preferred_element_type=F32)


def _gated_outproj(y, g, x, mod3, w_bf16, tm=256):
    bsz, seq, d = x.shape
    n = y.shape[-1]
    gate_spec = _mod_specs(d)[2]
    return pl.pallas_call(
        _outproj_kernel,
        out_shape=jax.ShapeDtypeStruct((bsz, seq, d), F32),
        grid=(bsz, seq // tm),
        in_specs=[pl.BlockSpec((1, tm, n), lambda b, t: (b, t, 0)),
                  pl.BlockSpec((1, tm, n), lambda b, t: (b, t, 0)),
                  pl.BlockSpec((1, tm, d), lambda b, t: (b, t, 0)),
                  gate_spec,
                  pl.BlockSpec((n, d), lambda b, t: (0, 0))],
        out_specs=pl.BlockSpec((1, tm, d), lambda b, t: (b, t, 0)),
        compiler_params=_params("parallel", "parallel"),
        name="gated_outproj",
    )(y, g, x, mod3, w_bf16)


def _head_group_ones():
    r = jnp.arange(LANES) // DSA_HEAD_DIM
    return (r[:, None] == r[None, :]).astype(BF16)


def _rope_tables(seq, dim, reps):
    half = dim // 2
    inv_freq = ROPE_THETA ** (-jnp.arange(half, dtype=F32) / half)
    ang = jnp.arange(seq, dtype=F32)[:, None] * inv_freq[None, :]
    cos = jnp.concatenate([jnp.cos(ang), jnp.cos(ang)], axis=1)
    sin = jnp.concatenate([-jnp.sin(ang), jnp.sin(ang)], axis=1)
    return jnp.tile(cos, (1, reps)), jnp.tile(sin, (1, reps))


def _rope64(x, cos, sin, lane_lo):
    cols = []
    for c in range(x.shape[1] // LANES):
        xb = x[:, c * LANES:(c + 1) * LANES]
        rot = jnp.where(lane_lo, pltpu.roll(xb, 96, 1), pltpu.roll(xb, 32, 1))
        cols.append(xb * cos + rot * sin)
    return cols[0] if len(cols) == 1 else jnp.concatenate(cols, axis=1)


def _rope128(x, cos, sin):
    cols = []
    for c in range(x.shape[1] // LANES):
        xb = x[:, c * LANES:(c + 1) * LANES]
        cols.append(xb * cos + pltpu.roll(xb, 64, 1) * sin)
    return cols[0] if len(cols) == 1 else jnp.concatenate(cols, axis=1)


_DSA_Q = DSA_HEADS * DSA_HEAD_DIM
_DSA_KV2 = DSA_KV_HEADS * LANES
_DSA_QI = DSA_IDX_HEADS * DSA_IDX_DIM
_DSA_OFF_Q = 0
_DSA_OFF_G = _DSA_OFF_Q + _DSA_Q
_DSA_OFF_QI = _DSA_OFF_G + _DSA_Q
_DSA_OFF_K = _DSA_OFF_QI + _DSA_QI
_DSA_OFF_V = _DSA_OFF_K + _DSA_KV2
_DSA_OFF_KI = _DSA_OFF_V + _DSA_KV2
_DSA_OFF_WI = _DSA_OFF_KI + DSA_IDX_DIM
_DSA_COLS = _DSA_OFF_WI + LANES


def _dsa_proj_kernel(x_ref, gain_ref, shift_ref, scale_ref, w_ref, qgain_ref, kgain_ref,
                     cos64_ref, sin64_ref, cos128_ref, sin128_ref, g01_ref,
                     q_ref, g_ref, qi_ref, kt_ref, v_ref, ki_ref, wi_ref):
    h = _prenorm(x_ref[0], gain_ref[...], scale_ref[0], shift_ref[0]).astype(BF16)
    g01 = g01_ref[...]
    cos64, sin64 = cos64_ref[...], sin64_ref[...]
    cos128, sin128 = cos128_ref[...], sin128_ref[...]
    lane = lax.broadcasted_iota(jnp.int32, cos64.shape, 1)
    lane_lo = (lane % DSA_HEAD_DIM) < DSA_HEAD_DIM // 2
    lane_lo64 = lane < DSA_HEAD_DIM

    def head_norm_rope(raw, gain):
        ms = _group_sum(raw * raw, g01) * (1.0 / DSA_HEAD_DIM)
        return _rope64(raw * lax.rsqrt(ms + RMS_EPS) * gain, cos64, sin64, lane_lo)

    def col(off, n):
        return jnp.dot(h, w_ref[:, off:off + n], preferred_element_type=F32)

    q = head_norm_rope(col(_DSA_OFF_Q, _DSA_Q), qgain_ref[...]) * (LOG2E * DSA_HEAD_DIM ** -0.5)
    for hh in range(DSA_HEADS):
        qb = q[:, (hh // 2) * LANES:(hh // 2 + 1) * LANES]
        keep = lane_lo64 if hh % 2 == 0 else ~lane_lo64
        q_ref[0, hh] = jnp.where(keep, qb, 0.0).astype(BF16)
    g_ref[0] = col(_DSA_OFF_G, _DSA_Q)
    qi = _rope128(col(_DSA_OFF_QI, _DSA_QI), cos128, sin128).astype(BF16)
    for hh in range(DSA_IDX_HEADS):
        qi_ref[0, hh] = qi[:, hh * DSA_IDX_DIM:(hh + 1) * DSA_IDX_DIM]
    k2 = head_norm_rope(col(_DSA_OFF_K, _DSA_KV2), kgain_ref[...])
    kt = k2.T.astype(BF16)
    for kv in range(DSA_KV_HEADS):
        kt_ref[0, kv] = kt[kv * LANES:(kv + 1) * LANES, :]
    v2 = col(_DSA_OFF_V, _DSA_KV2)
    for kv in range(DSA_KV_HEADS):
        v_ref[0, kv] = jnp.where(lane_lo64, v2[:, kv * LANES:(kv + 1) * LANES], 1.0).astype(BF16)
    ki = _rope128(col(_DSA_OFF_KI, DSA_IDX_DIM), cos128, sin128)
    ki_ref[0] = ki.T.astype(BF16)
    wi_ref[0] = col(_DSA_OFF_WI, LANES) * DSA_IDX_SCALE


def _dsa_weights(w_in):
    d = w_in.shape[0]
    q_end = _DSA_Q
    k_end = q_end + DSA_KV_HEADS * DSA_HEAD_DIM
    v_end = k_end + DSA_KV_HEADS * DSA_HEAD_DIM
    g_end = v_end + _DSA_Q
    qi_end = g_end + _DSA_QI
    wi_end = qi_end + DSA_IDX_HEADS
    wq, wk, wv, wg = w_in[:, :q_end], w_in[:, q_end:k_end], w_in[:, k_end:v_end], w_in[:, v_end:g_end]
    wqi, wwi, wki = w_in[:, g_end:qi_end], w_in[:, qi_end:wi_end], w_in[:, wi_end:]

    def dup(w):
        w = w.reshape(d, DSA_KV_HEADS, 1, DSA_HEAD_DIM)
        return jnp.broadcast_to(w, (d, DSA_KV_HEADS, 2, DSA_HEAD_DIM)).reshape(d, _DSA_KV2)

    wwi = jnp.pad(wwi, ((0, 0), (0, LANES - DSA_IDX_HEADS)))
    return jnp.concatenate([wq, wg, wqi, dup(wk), dup(wv), wki, wwi], axis=1).astype(BF16)


def _dsa_project(x, gain, mod3, w_in, q_gain, k_gain, tm=256):
    bsz, seq, d = x.shape
    w = _dsa_weights(w_in)
    cos64, sin64 = _rope_tables(seq, DSA_HEAD_DIM, LANES // DSA_HEAD_DIM)
    cos128, sin128 = _rope_tables(seq, DSA_IDX_DIM, 1)
    shift_spec, scale_spec, _ = _mod_specs(d)
    full = lambda shape: pl.BlockSpec(shape, lambda b, t: (0,) * len(shape))
    table = pl.BlockSpec((tm, LANES), lambda b, t: (t, 0))
    row = lambda n, dt: jax.ShapeDtypeStruct((bsz, seq, n), dt)
    heads = lambda n: jax.ShapeDtypeStruct((bsz, n, seq, LANES), BF16)
    head_spec = lambda n: pl.BlockSpec((1, n, tm, LANES), lambda b, t: (b, 0, t, 0))
    return pl.pallas_call(
        _dsa_proj_kernel,
        out_shape=[heads(DSA_HEADS), row(_DSA_Q, F32), heads(DSA_IDX_HEADS),
                   jax.ShapeDtypeStruct((bsz, DSA_KV_HEADS, LANES, seq), BF16),
                   heads(DSA_KV_HEADS),
                   jax.ShapeDtypeStruct((bsz, DSA_IDX_DIM, seq), BF16), row(LANES, F32)],
        grid=(bsz, seq // tm),
        in_specs=[pl.BlockSpec((1, tm, d), lambda b, t: (b, t, 0)),
                  full((1, d)), shift_spec, scale_spec, full(w.shape),
                  full((1, _DSA_Q)), full((1, _DSA_KV2)),
                  table, table, table, table, full((LANES, LANES))],
        out_specs=[head_spec(DSA_HEADS),
                   pl.BlockSpec((1, tm, _DSA_Q), lambda b, t: (b, t, 0)),
                   head_spec(DSA_IDX_HEADS),
                   pl.BlockSpec((1, DSA_KV_HEADS, LANES, tm), lambda b, t: (b, 0, 0, t)),
                   head_spec(DSA_KV_HEADS),
                   pl.BlockSpec((1, DSA_IDX_DIM, tm), lambda b, t: (b, 0, t)),
                   pl.BlockSpec((1, tm, LANES), lambda b, t: (b, t, 0))],
        compiler_params=_params("parallel", "parallel"),
        name="dsa_proj",
    )(x, gain.reshape(1, d), mod3, mod3, w,
      jnp.tile(q_gain, DSA_HEADS).reshape(1, _DSA_Q),
      jnp.tile(k_gain, _DSA_KV2 // DSA_HEAD_DIM).reshape(1, _DSA_KV2),
      cos64, sin64, cos128, sin128, _head_group_ones())


def _sortable_to_float(key):
    bits = jnp.where(key >= 0, key, key ^ jnp.int32(0x7FFFFFFF))
    return lax.bitcast_convert_type(bits, F32)


def _dsa_attn_kernel(n_sel, chunk, qi_ref, wi_ref, ki_ref, q_ref, kt_ref, v_ref, tri_ref, o_ref,
                     score_ref, bias_ref):
    qb, seq = score_ref.shape
    per = chunk // qb
    needed = lax.div(pl.program_id(1) + per, per)
    for j in range(seq // chunk):
        pl.when(needed == j + 1)(functools.partial(
            _dsa_attn_block, n_sel, chunk * (j + 1), qi_ref, wi_ref, ki_ref, q_ref, kt_ref, v_ref,
            tri_ref, o_ref, score_ref, bias_ref))


def _dsa_attn_block(n_sel, width, qi_ref, wi_ref, ki_ref, q_ref, kt_ref, v_ref, tri_ref, o_ref,
                    score_ref, bias_ref):
    blk = pl.program_id(1)
    qb = score_ref.shape[0]
    seq = width
    kit = ki_ref[0, :, 0:width]
    wi = wi_ref[0]
    stacked = DSA_HEADS // DSA_KV_HEADS

    acc = None
    for grp in range(DSA_IDX_HEADS // stacked):
        qs = qi_ref[0, grp * stacked:(grp + 1) * stacked].reshape(stacked * qb, DSA_IDX_DIM)
        logits = jnp.dot(qs, kit, preferred_element_type=F32)
        for j in range(stacked):
            hh = grp * stacked + j
            term = wi[:, hh:hh + 1] * jnp.maximum(logits[j * qb:(j + 1) * qb], 0.0)
            acc = term if acc is None else acc + term
    q_pos = blk * qb + lax.broadcasted_iota(jnp.int32, (qb, 1), 0)
    key_pos = lax.broadcasted_iota(jnp.int32, (1, seq), 1)
    score_ref[:, 0:width] = jnp.where(key_pos <= q_pos, acc, -jnp.inf)

    k_sel = jnp.float32(n_sel)

    def count_ge(thr):
        return jnp.sum(jnp.where(score_ref[:, 0:width] >= thr, 1.0, 0.0), axis=1, keepdims=True)

    def reaches(cand):
        return jnp.where(count_ge(_sortable_to_float(cand)) >= k_sel, 1, 0)

    int_min = jnp.int32(-2 ** 31)
    key0 = jnp.where(count_ge(jnp.zeros((qb, 1), F32)) >= k_sel, jnp.int32(0), int_min)

    def two_bits(i, key):
        unit = lax.shift_left(jnp.int32(1), jnp.int32(29) - 2 * i)
        taken = reaches(key + unit) + reaches(key + 2 * unit) + reaches(key + 3 * unit)
        return key + taken * unit

    key = lax.fori_loop(0, 15, two_bits, key0)
    thr = _sortable_to_float(key + reaches(key + 1))

    score = score_ref[:, 0:width]
    gt = score > thr
    need = k_sel - jnp.sum(jnp.where(gt, 1.0, 0.0), axis=1, keepdims=True)
    take_all = q_pos < n_sel
    tri = tri_ref[...]
    run = jnp.zeros((qb, 1), F32)
    for c in range(seq // LANES):
        sl = slice(c * LANES, (c + 1) * LANES)
        eq = jnp.where(score[:, sl] == thr, 1.0, 0.0)
        incl = jnp.dot(eq.astype(BF16), tri, preferred_element_type=F32)
        tie_ok = (incl - eq + run) < need
        run = run + incl[:, LANES - 1:LANES]
        sel = gt[:, sl] | ((eq > 0.0) & tie_ok) | take_all
        causal = key_pos[:, sl] <= q_pos
        bias_ref[:, sl] = jnp.where(sel & causal, 0.0, NEG_BIG)

    lane = lax.broadcasted_iota(jnp.int32, (qb, LANES), 1)
    lo = lane < DSA_HEAD_DIM
    bias = bias_ref[:, 0:width]
    for kv in range(DSA_KV_HEADS):
        q4 = q_ref[0, kv * stacked:(kv + 1) * stacked].reshape(stacked * qb, LANES)
        s4 = jnp.dot(q4, kt_ref[0, kv, :, 0:width], preferred_element_type=F32)
        ps = []
        for j in range(stacked):
            s = s4[j * qb:(j + 1) * qb] + bias
            ps.append(jnp.exp2(s - jnp.max(s, axis=1, keepdims=True)).astype(BF16))
        ov = jnp.dot(jnp.concatenate(ps, axis=0), v_ref[0, kv, 0:width, :],
                     preferred_element_type=F32)
        ov = ov / pltpu.roll(ov, DSA_HEAD_DIM, 1)
        for c in range(stacked // 2):
            even = ov[(2 * c) * qb:(2 * c + 1) * qb]
            odd = ov[(2 * c + 1) * qb:(2 * c + 2) * qb]
            pair = kv * (stacked // 2) + c
            o_ref[0, :, pair * LANES:(pair + 1) * LANES] = jnp.where(
                lo, even, pltpu.roll(odd, DSA_HEAD_DIM, 1))


def _dsa_attention(q, qi, kt2, v2, ki, wi):
    bsz, _, seq, _ = q.shape
    qb = DSA_QBLOCK
    n_sel = min(DSA_TOPK, seq // 4)
    chunk = min(DSA_KEY_CHUNK, seq // 2)
    tri = (jnp.arange(LANES)[:, None] <= jnp.arange(LANES)[None, :]).astype(BF16)
    return pl.pallas_call(
        functools.partial(_dsa_attn_kernel, n_sel, chunk),
        out_shape=jax.ShapeDtypeStruct((bsz, seq, _DSA_Q), F32),
        grid=(bsz, seq // qb),
        in_specs=[pl.BlockSpec((1, DSA_IDX_HEADS, qb, LANES), lambda b, i: (b, 0, i, 0)),
                  pl.BlockSpec((1, qb, LANES), lambda b, i: (b, i, 0)),
                  pl.BlockSpec((1, DSA_IDX_DIM, seq), lambda b, i: (b, 0, 0)),
                  pl.BlockSpec((1, DSA_HEADS, qb, LANES), lambda b, i: (b, 0, i, 0)),
                  pl.BlockSpec((1, DSA_KV_HEADS, LANES, seq), lambda b, i: (b, 0, 0, 0)),
                  pl.BlockSpec((1, DSA_KV_HEADS, seq, LANES), lambda b, i: (b, 0, 0, 0)),
                  pl.BlockSpec((LANES, LANES), lambda b, i: (0, 0))],
        out_specs=pl.BlockSpec((1, qb, _DSA_Q), lambda b, i: (b, i, 0)),
        scratch_shapes=[pltpu.VMEM((qb, seq), F32), pltpu.VMEM((qb, seq), F32)],
        compiler_params=_params("parallel", "parallel"),
        name="dsa_attn",
    )(qi, wi, ki, q, kt2, v2, tri)


def _dsa_layer(x, gain, mod3, w_in, q_gain, k_gain, w_out):
    q, g, qi, kt2, v2, ki, wi = _dsa_project(x, gain, mod3, w_in, q_gain, k_gain)
    o = _dsa_attention(q, qi, kt2, v2, ki, wi)
    return _gated_outproj(o, g, x, mod3, w_out.astype(BF16))


def _lru_kernel(u_ref, cw_ref, cb_ref, wa_ref, ba_ref, wx_ref, bx_ref, lam_ref, o_ref,
                ubuf, a_buf, b_buf, h_carry):
    tt = u_ref.shape[1]
    width = u_ref.shape[2]
    halo = SUBLANES

    @pl.when(pl.program_id(1) == 0)
    def _():
        ubuf[0:halo, :] = jnp.zeros((halo, width), F32)
        h_carry[...] = jnp.zeros_like(h_carry)

    ubuf[halo:halo + tt, :] = u_ref[0]
    cw = cw_ref[...]
    u = cb_ref[...]
    for j in range(LRU_CONV):
        start = halo - (LRU_CONV - 1) + j
        u = u + cw[j:j + 1, :] * ubuf[start:start + tt, :]
    ubuf[0:halo, :] = ubuf[tt:tt + halo, :]

    sp = _softplus(-lam_ref[...])
    for c in range(width // LRU_GROUP):
        sl = slice(c * LRU_GROUP, (c + 1) * LRU_GROUP)
        uc = u[:, sl]
        ub = uc.astype(BF16)
        r = jax.nn.sigmoid(jnp.dot(ub, wa_ref[c], preferred_element_type=F32) + ba_ref[:, sl])
        i = jax.nn.sigmoid(jnp.dot(ub, wx_ref[c], preferred_element_type=F32) + bx_ref[:, sl])
        a = jnp.exp(-LRU_C * r * sp[:, sl])
        a_buf[:, sl] = a
        b_buf[:, sl] = jnp.sqrt(1.0 - a * a) * (i * uc)

    row = lax.broadcasted_iota(jnp.int32, (SUBLANES, width), 0)

    def group(gi, h_prev):
        r0 = pl.multiple_of(gi * SUBLANES, SUBLANES)
        a = a_buf[pl.ds(r0, SUBLANES), :]
        b = b_buf[pl.ds(r0, SUBLANES), :]
        for s in (1, 2, 4):
            ok = row >= s
            b = jnp.where(ok, a * pltpu.roll(b, s, 0) + b, b)
            a = jnp.where(ok, a * pltpu.roll(a, s, 0), a)
        h = a * h_prev + b
        o_ref[0, pl.ds(r0, SUBLANES), :] = h
        return jnp.broadcast_to(h[SUBLANES - 1:SUBLANES, :], (SUBLANES, width))

    h_carry[...] = lax.fori_loop(0, tt // SUBLANES, group, h_carry[...])


def _lru_gate_blocks(w):
    nb, bd, _ = w.shape
    per = LRU_GROUP // bd
    w = w.reshape(nb // per, per, bd, bd)
    eye = jnp.eye(per, dtype=w.dtype)
    return jnp.einsum('gpcd,pq->gpcqd', w, eye).reshape(nb // per, LRU_GROUP, LRU_GROUP).astype(BF16)


def _lru_scan(u_raw, conv_w, conv_b, gate_a_w, gate_a_b, gate_x_w, gate_x_b, lam, tt=256):
    bsz, seq, width = u_raw.shape
    ngroups = width // LRU_GROUP
    vec = lambda: pl.BlockSpec((1, width), lambda b, t: (0, 0))
    gatew = lambda: pl.BlockSpec((ngroups, LRU_GROUP, LRU_GROUP), lambda b, t: (0, 0, 0))
    cw = jnp.pad(conv_w, ((0, SUBLANES - LRU_CONV), (0, 0)))
    return pl.pallas_call(
        _lru_kernel,
        out_shape=jax.ShapeDtypeStruct((bsz, seq, width), F32),
        grid=(bsz, seq // tt),
        in_specs=[pl.BlockSpec((1, tt, width), lambda b, t: (b, t, 0)),
                  pl.BlockSpec((SUBLANES, width), lambda b, t: (0, 0)),
                  vec(), gatew(), vec(), gatew(), vec(), vec()],
        out_specs=pl.BlockSpec((1, tt, width), lambda b, t: (b, t, 0)),
        scratch_shapes=[pltpu.VMEM((tt + SUBLANES, width), F32),
                        pltpu.VMEM((tt, width), F32), pltpu.VMEM((tt, width), F32),
                        pltpu.VMEM((SUBLANES, width), F32)],
        compiler_params=_params("parallel", "arbitrary"),
        name="rglru_scan",
    )(u_raw, cw, conv_b.reshape(1, width), _lru_gate_blocks(gate_a_w), gate_a_b.reshape(1, width),
      _lru_gate_blocks(gate_x_w), gate_x_b.reshape(1, width), lam.reshape(1, width))


def _lru_layer(x, gain, mod3, w_in, conv_w, conv_b, gate_a_w, gate_a_b, gate_x_w, gate_x_b, lam, w_out):
    width = w_in.shape[1] // 2
    u_raw, g = _norm_proj(x, gain, mod3, w_in.astype(BF16), (width, width))
    hs = _lru_scan(u_raw, conv_w, conv_b, gate_a_w, gate_a_b, gate_x_w, gate_x_b, lam)
    return _gated_outproj(hs, g, x, mod3, w_out.astype(BF16))


def _rwkv_proj_kernel(x_ref, gain_ref, shift_ref, scale_ref, mu_ref, w_ref, w1_ref, w2_ref,
                      a1_ref, a2_ref, w0_ref, a0_ref, kk_ref, ka_ref, rk_ref, g01_ref,
                      r_out, lw_out, k_out, v_out, kk_out, a_out, g_out, bonus_out, carry):
    tm = x_ref.shape[1]
    d = x_ref.shape[2]

    @pl.when(pl.program_id(1) == 0)
    def _():
        carry[...] = jnp.zeros_like(carry)

    h = _prenorm(x_ref[0], gain_ref[...], scale_ref[0], shift_ref[0])
    first = lax.broadcasted_iota(jnp.int32, (tm, d), 0) == 0
    h_prev = jnp.where(first, carry[0:1, :], pltpu.roll(h, 1, 0))
    carry[...] = jnp.broadcast_to(h[tm - 1:tm, :], carry.shape)
    delta = h_prev - h
    mu = mu_ref[...]

    def mix(n):
        return (h + delta * mu[n:n + 1, :]).astype(BF16)

    r = jnp.dot(mix(0), w_ref[0], preferred_element_type=F32)
    k = jnp.dot(mix(1), w_ref[1], preferred_element_type=F32)
    v = jnp.dot(mix(2), w_ref[2], preferred_element_type=F32)
    g_out[0] = jnp.dot(mix(3), w_ref[3], preferred_element_type=F32)
    w_lora = _dot(jnp.tanh(jnp.dot(mix(4), w1_ref[...], preferred_element_type=F32)), w2_ref[...])
    w_log = -_softplus(-(w0_ref[...] + w_lora)) - 0.5
    lw_out[0] = -jnp.exp(w_log)
    a_lora = _dot(jnp.dot(mix(5), a1_ref[...], preferred_element_type=F32), a2_ref[...])
    a = jax.nn.sigmoid(a0_ref[...] + a_lora)
    g01 = g01_ref[...]
    kk = k * kk_ref[...]
    kk = kk / jnp.maximum(jnp.sqrt(_group_sum(kk * kk, g01)), 1e-12)
    k = k * (1.0 + (a - 1.0) * ka_ref[...])
    r_out[0] = r
    k_out[0] = k
    v_out[0] = v
    kk_out[0] = kk
    a_out[0] = a
    bonus_out[0] = _group_sum(r * k * rk_ref[...], g01) * v


def _rwkv_project(x, gain, mod3, mu, w_in, w0, w1, w2, a0, a1, a2, k_k, k_a, r_k, tm=256):
    bsz, seq, d = x.shape
    shift_spec, scale_spec, _ = _mod_specs(d)
    full = lambda shape: pl.BlockSpec(shape, lambda b, t: (0,) * len(shape))
    rank = w1.shape[1]
    pad_c = lambda w: jnp.pad(w, ((0, 0), (0, LANES - rank))).astype(BF16)
    pad_r = lambda w: jnp.pad(w, ((0, LANES - rank), (0, 0))).astype(BF16)
    vec = lambda p: p.reshape(1, d)
    out = jax.ShapeDtypeStruct((bsz, seq, d), F32)
    tile = pl.BlockSpec((1, tm, d), lambda b, t: (b, t, 0))
    return pl.pallas_call(
        _rwkv_proj_kernel,
        out_shape=[out] * 8,
        grid=(bsz, seq // tm),
        in_specs=[tile, full((1, d)), shift_spec, scale_spec, full((SUBLANES, d)),
                  full((4, d, d)), full((d, LANES)), full((LANES, d)), full((d, LANES)),
                  full((LANES, d)), full((1, d)), full((1, d)), full((1, d)), full((1, d)),
                  full((1, d)), full((LANES, LANES))],
        out_specs=[tile] * 8,
        scratch_shapes=[pltpu.VMEM((SUBLANES, d), F32)],
        compiler_params=_params("parallel", "arbitrary"),
        name="rwkv_proj",
    )(x, gain.reshape(1, d), mod3, mod3, jnp.pad(mu, ((0, SUBLANES - mu.shape[0]), (0, 0))),
      w_in.astype(BF16), pad_c(w1), pad_r(w2), pad_c(a1), pad_r(a2), vec(w0), vec(a0),
      vec(k_k), vec(k_a), vec(r_k), _head_group_ones())


def _rwkv_chunk_kernel(r_ref, lw_ref, k_ref, v_ref, kk_ref, a_ref, bonus_ref, lnw_ref, lnb_ref,
                       g01_ref, ltri_ref, o_ref, state):
    tc = r_ref.shape[1]
    d = r_ref.shape[2]
    cs = RWKV_CHUNK
    hd = RWKV_HEAD_DIM

    @pl.when(pl.program_id(1) == 0)
    def _():
        state[...] = jnp.zeros_like(state)

    lane = lax.broadcasted_iota(jnp.int32, (cs, LANES), 1)
    lo = lane < hd
    ri = lax.broadcasted_iota(jnp.int32, (LANES, LANES), 0)
    ci = lax.broadcasted_iota(jnp.int32, (LANES, LANES), 1)
    same = (ri < cs) == (ci < cs)
    strict = same & (ci < ri)
    incl = same & (ci <= ri)
    eye = jnp.where(ri == ci, 1.0, 0.0)
    g01 = g01_ref[...]
    ltri = ltri_ref[...]

    def stack(z):
        return jnp.concatenate([z, z], axis=0)

    def split_heads(z):
        return jnp.concatenate([jnp.where(lo, z, 0.0), jnp.where(lo, 0.0, z)], axis=0)

    def own(z):
        return jnp.where(lo, z[0:cs], z[cs:2 * cs])

    pairs = range(d // LANES)
    lanes = [slice(p * LANES, (p + 1) * LANES) for p in pairs]

    for c in range(tc // cs):
        rows = slice(c * cs, (c + 1) * cs)
        lw = lw_ref[0, rows, :]
        g_inc = _dot_exact_lhs(ltri, lw)
        g_last = g_inc[cs - 1:cs, :]
        kk = kk_ref[0, rows, :]
        kc = k_ref[0, rows, :]
        vc = v_ref[0, rows, :]
        bvec = kk * a_ref[0, rows, :]
        e_neg = jnp.exp(-g_inc)
        e_end = jnp.exp(g_last - g_inc)
        at = -kk * jnp.exp(g_inc - lw)
        rt = r_ref[0, rows, :] * jnp.exp(g_inc)
        bt = bvec * e_neg
        kt = kc * e_neg
        bend = (bvec * e_end).astype(BF16)
        kend = (kc * e_end).astype(BF16)
        decay = jnp.exp(g_last)

        lhs = [jnp.concatenate([at[:, s], at[:, s], rt[:, s], rt[:, s]], axis=0).astype(BF16)
               for s in lanes]
        nb = [_dot_nt(lhs[p], split_heads(bt[:, lanes[p]])) for p in pairs]
        nk = [_dot_nt(lhs[p], split_heads(kt[:, lanes[p]])) for p in pairs]
        n_ab = [jnp.where(strict, nb[p][0:LANES], 0.0) for p in pairs]
        n_rb = [jnp.where(incl, nb[p][LANES:2 * LANES], 0.0).astype(BF16) for p in pairs]
        n_ak = [jnp.where(strict, nk[p][0:LANES], 0.0) for p in pairs]
        n_rk = [jnp.where(incl, nk[p][LANES:2 * LANES], 0.0) for p in pairs]

        inv = [eye + n_ab[p] for p in pairs]
        pw = [_dot(n_ab[p], n_ab[p]) for p in pairs]
        for stage in range(4):
            res = [_dot(pw[p], jnp.concatenate([pw[p], inv[p]], axis=1)) for p in pairs]
            pw = [res[p][:, 0:LANES] for p in pairs]
            inv = [inv[p] + res[p][:, LANES:2 * LANES] for p in pairs]
        inv = [(inv[p] + _dot(pw[p], inv[p])).astype(BF16) for p in pairs]

        vv = [stack(vc[:, s]).astype(BF16) for s in lanes]
        w1 = [own(_dot(n_ak[p], vv[p])) for p in pairs]
        tz = [_dot(inv[p], jnp.concatenate([stack(at[:, lanes[p]]), stack(w1[p])], axis=1))
              for p in pairs]
        a2 = [own(tz[p][:, 0:LANES]) for p in pairs]
        u0 = [own(tz[p][:, LANES:2 * LANES]) for p in pairs]
        rz = [_dot(n_rb[p], jnp.concatenate([stack(a2[p]), stack(u0[p])], axis=1)) for p in pairs]
        r2 = [rt[:, lanes[p]] + own(rz[p][:, 0:LANES]) for p in pairs]
        y0 = [own(rz[p][:, LANES:2 * LANES]) + own(_dot(n_rk[p], vv[p])) for p in pairs]
        mlr = [jnp.where(same, _dot(a2[p].T, bend[:, lanes[p]]), 0.0) for p in pairs]
        c0 = [jnp.where(same, _dot(jnp.concatenate([u0[p], vc[:, lanes[p]]], axis=0).T,
                                   jnp.concatenate([bend[:, lanes[p]], kend[:, lanes[p]]], axis=0)), 0.0)
              for p in pairs]

        ys = []
        for p in pairs:
            s_bd = state[p]
            ys.append(_dot_nt(r2[p], s_bd) + y0[p])
            state[p] = s_bd * decay[:, lanes[p]] + _dot(s_bd, mlr[p]) + c0[p]
        y = jnp.concatenate(ys, axis=1)
        mean = _group_sum(y, g01) * (1.0 / hd)
        yc = y - mean
        var = _group_sum(yc * yc, g01) * (1.0 / hd)
        o_ref[0, rows, :] = (yc * lax.rsqrt(var + RWKV_GN_EPS) * lnw_ref[...] + lnb_ref[...]
                             + bonus_ref[0, rows, :])


def _rwkv_recurrence(r, lw, k, v, kk, a, bonus, ln_w, ln_b, tc=128):
    bsz, seq, d = r.shape
    tile = pl.BlockSpec((1, tc, d), lambda b, t: (b, t, 0))
    vec = pl.BlockSpec((1, d), lambda b, t: (0, 0))
    ltri = (jnp.arange(RWKV_CHUNK)[:, None] >= jnp.arange(RWKV_CHUNK)[None, :]).astype(BF16)
    return pl.pallas_call(
        _rwkv_chunk_kernel,
        out_shape=jax.ShapeDtypeStruct((bsz, seq, d), F32),
        grid=(bsz, seq // tc),
        in_specs=[tile] * 7 + [vec, vec,
                               pl.BlockSpec((LANES, LANES), lambda b, t: (0, 0)),
                               pl.BlockSpec((RWKV_CHUNK, RWKV_CHUNK), lambda b, t: (0, 0))],
        out_specs=tile,
        scratch_shapes=[pltpu.VMEM((d // LANES, LANES, LANES), F32)],
        compiler_params=_params("parallel", "arbitrary"),
        name="rwkv_chunk",
    )(r, lw, k, v, kk, a, bonus, ln_w.reshape(1, d), ln_b.reshape(1, d), _head_group_ones(), ltri)


def _rwkv_layer(x, gain, mod3, mu, w_in, w0, w1, w2, a0, a1, a2, k_k, k_a, r_k, ln_w, ln_b, w_out):
    r, lw, k, v, kk, a, g, bonus = _rwkv_project(x, gain, mod3, mu, w_in, w0, w1, w2, a0, a1, a2,
                                                 k_k, k_a, r_k)
    y = _rwkv_recurrence(r, lw, k, v, kk, a, bonus, ln_w, ln_b)
    return _gated_outproj(y, g, x, mod3, w_out.astype(BF16))


def _gla_kernel(q_ref, k_ref, v_ref, alow_ref, w2_ref, ab_ref, gain_ref, ltri_ref, o_ref, state):
    tt = q_ref.shape[1]
    key_dim = q_ref.shape[2]
    dk = key_dim // GLA_HEADS
    dv = v_ref.shape[2] // GLA_HEADS
    cs = GLA_CHUNK

    @pl.when(pl.program_id(1) == 0)
    def _():
        state[...] = jnp.zeros_like(state)

    ri = lax.broadcasted_iota(jnp.int32, (cs, cs), 0)
    ci = lax.broadcasted_iota(jnp.int32, (cs, cs), 1)
    causal = ci <= ri
    ltri = ltri_ref[...]
    z = _dot(alow_ref[0], w2_ref[...]) + ab_ref[...]
    log_alpha = -_softplus(-z) * (1.0 / GLA_GATE_NORM)

    for c in range(tt // cs):
        rows = slice(c * cs, (c + 1) * cs)
        cum = _dot_exact_lhs(ltri, log_alpha[rows, :])
        last = cum[cs - 1:cs, :]
        kc = k_ref[0, rows, :]
        q_dec = q_ref[0, rows, :] * (dk ** -0.5) * jnp.exp(cum)
        k_inv = kc * jnp.exp(-cum)
        k_end = kc * jnp.exp(last - cum)
        decay = jnp.exp(last)
        for hh in range(GLA_HEADS):
            ks = slice(hh * dk, (hh + 1) * dk)
            vs = slice(hh * dv, (hh + 1) * dv)
            vh = v_ref[0, rows, vs]
            st = state[hh]
            att = jnp.where(causal, _dot_nt(q_dec[:, ks], k_inv[:, ks]), 0.0)
            o = _dot(att, vh) + _dot_nt(q_dec[:, ks], st)
            state[hh] = st * decay[:, ks] + _dot(vh.T, k_end[:, ks])
            ms = jnp.mean(o * o, axis=-1, keepdims=True)
            o_ref[0, rows, vs] = o * lax.rsqrt(ms + RMS_EPS) * gain_ref[...]


def _gla_core(q, k, v, a_low, alpha_w2, alpha_b, norm_gain, tt=256):
    bsz, seq, key_dim = q.shape
    val_dim = v.shape[2]
    dk, dv = key_dim // GLA_HEADS, val_dim // GLA_HEADS
    w2 = jnp.pad(alpha_w2, ((0, LANES - GLA_GATE_RANK), (0, 0))).astype(BF16)
    ltri = (jnp.arange(GLA_CHUNK)[:, None] >= jnp.arange(GLA_CHUNK)[None, :]).astype(BF16)
    full = lambda shape: pl.BlockSpec(shape, lambda b, t: (0,) * len(shape))
    return pl.pallas_call(
        _gla_kernel,
        out_shape=jax.ShapeDtypeStruct((bsz, seq, val_dim), F32),
        grid=(bsz, seq // tt),
        in_specs=[pl.BlockSpec((1, tt, key_dim), lambda b, t: (b, t, 0)),
                  pl.BlockSpec((1, tt, key_dim), lambda b, t: (b, t, 0)),
                  pl.BlockSpec((1, tt, val_dim), lambda b, t: (b, t, 0)),
                  pl.BlockSpec((1, tt, LANES), lambda b, t: (b, t, 0)),
                  full((LANES, key_dim)), full((1, key_dim)), full((1, dv)),
                  full((GLA_CHUNK, GLA_CHUNK))],
        out_specs=pl.BlockSpec((1, tt, val_dim), lambda b, t: (b, t, 0)),
        scratch_shapes=[pltpu.VMEM((GLA_HEADS, dv, dk), F32)],
        compiler_params=_params("parallel", "arbitrary"),
        name="gla_chunk",
    )(q, k, v, a_low, w2, alpha_b.reshape(1, key_dim), norm_gain.reshape(1, dv), ltri)


def _gla_layer(x, gain, mod3, w_in, alpha_w2, alpha_b, norm_gain, w_out):
    d = x.shape[2]
    key_dim, val_dim = alpha_w2.shape[1], w_out.shape[0]
    w = jnp.pad(w_in, ((0, 0), (0, LANES - GLA_GATE_RANK))).astype(BF16)
    q, k, v, g, a_low = _norm_proj(x, gain, mod3, w, (key_dim, key_dim, val_dim, val_dim, LANES))
    o = _gla_core(q, k, v, a_low, alpha_w2, alpha_b, norm_gain)
    return _gated_outproj(o, g, x, mod3, w_out.astype(BF16))


def kernel(x, c, ln_gain, mod_w, mod_b, dsa_w_in, dsa_q_gain, dsa_k_gain, dsa_w_out, lru_w_in, lru_conv_w, lru_conv_b, lru_gate_a_w, lru_gate_a_b, lru_gate_x_w, lru_gate_x_b, lru_lambda, lru_w_out, rwkv_mu, rwkv_w_in, rwkv_w0, rwkv_w1, rwkv_w2, rwkv_a0, rwkv_a1, rwkv_a2, rwkv_k_k, rwkv_k_a, rwkv_r_k, rwkv_ln_w, rwkv_ln_b, rwkv_w_out, gla_w_in, gla_alpha_w2, gla_alpha_b, gla_norm_gain, gla_w_out):
    depth = mod_w.shape[0]
    bsz, _, d = x.shape
    mod = _modulation(c, mod_w, mod_b)
    for layer in range(depth):
        mixer, r = layer % 4, layer // 4
        mod3 = mod[layer].reshape(bsz, 1, 3 * d)
        gain = ln_gain[layer]
        if mixer == 0:
            x = _dsa_layer(x, gain, mod3, dsa_w_in[r], dsa_q_gain[r], dsa_k_gain[r], dsa_w_out[r])
        elif mixer == 1:
            x = _lru_layer(x, gain, mod3, lru_w_in[r], lru_conv_w[r], lru_conv_b[r], lru_gate_a_w[r],
                           lru_gate_a_b[r], lru_gate_x_w[r], lru_gate_x_b[r], lru_lambda[r], lru_w_out[r])
        elif mixer == 2:
            x = _rwkv_layer(x, gain, mod3, rwkv_mu[r], rwkv_w_in[r], rwkv_w0[r], rwkv_w1[r], rwkv_w2[r],
                            rwkv_a0[r], rwkv_a1[r], rwkv_a2[r], rwkv_k_k[r], rwkv_k_a[r],
                            rwkv_r_k[r].reshape(-1), rwkv_ln_w[r], rwkv_ln_b[r], rwkv_w_out[r])
        else:
            x = _gla_layer(x, gain, mod3, gla_w_in[r], gla_alpha_w2[r], gla_alpha_b[r],
                           gla_norm_gain[r], gla_w_out[r])
    return x
```

```python
import functools

import jax
import jax.numpy as jnp
from jax import lax
from jax.experimental import pallas as pl
from jax.experimental.pallas import tpu as pltpu

F32 = jnp.float32
BF16 = jnp.bfloat16
HIGHEST = lax.Precision.HIGHEST

LANES = 128
SUBLANES = 8
VMEM_LIMIT_BYTES = 56 * 1024 * 1024

RMS_EPS = 1e-6
ROPE_THETA = 10000.0

DSA_HEADS = 16
DSA_KV_HEADS = 4
DSA_HEAD_DIM = 64
DSA_IDX_HEADS = 8
DSA_IDX_DIM = 128
DSA_TOPK = 256
DSA_QBLOCK = 128
DSA_KEY_CHUNK = 512
LOG2E = 1.4426950408889634
DSA_IDX_SCALE = (DSA_IDX_HEADS * DSA_IDX_DIM) ** -0.5

LRU_BLOCKS = 16
LRU_CONV = 4
LRU_C = 8.0
LRU_GROUP = 256

RWKV_HEAD_DIM = 64
RWKV_GN_EPS = 64e-5
RWKV_CHUNK = 64

GLA_HEADS = 4
GLA_GATE_RANK = 16
GLA_GATE_NORM = 16.0
GLA_CHUNK = 64

NEG_BIG = -1e30
NT_DIMS = (((1,), (1,)), ((), ()))


def _params(*semantics):
    return pltpu.CompilerParams(dimension_semantics=semantics,
                                vmem_limit_bytes=VMEM_LIMIT_BYTES)


def _resident(shape):
    return pl.BlockSpec(shape, lambda *_: (0,) * len(shape), pipeline_mode=pl.Buffered(1))


def _dot(a, b):
    return jnp.dot(a.astype(BF16), b.astype(BF16), preferred_element_type=F32)


def _dot_nt(a, b):
    return lax.dot_general(a.astype(BF16), b.astype(BF16), NT_DIMS,
                           preferred_element_type=F32)


def _split3(x):
    hi = x.astype(BF16)
    r1 = x - hi.astype(F32)
    mid = r1.astype(BF16)
    lo = (r1 - mid.astype(F32)).astype(BF16)
    return hi, mid, lo


def _dot_exact_lhs(m01, x):
    hi, mid, lo = _split3(x)
    return (jnp.dot(m01, hi, preferred_element_type=F32)
            + jnp.dot(m01, mid, preferred_element_type=F32)
            + jnp.dot(m01, lo, preferred_element_type=F32))


def _dot_exact_rhs(x, m01):
    hi, mid, lo = _split3(x)
    return (jnp.dot(hi, m01, preferred_element_type=F32)
            + jnp.dot(mid, m01, preferred_element_type=F32)
            + jnp.dot(lo, m01, preferred_element_type=F32))


def _group_sum(z, g01):
    cols = [_dot_exact_rhs(z[:, c * LANES:(c + 1) * LANES], g01)
            for c in range(z.shape[1] // LANES)]
    return cols[0] if len(cols) == 1 else jnp.concatenate(cols, axis=1)


def _group_sum_lanes(z, lo):
    cols = []
    for c in range(z.shape[1] // LANES):
        zb = z[:, c * LANES:(c + 1) * LANES]
        s_lo = jnp.sum(jnp.where(lo, zb, 0.0), axis=1, keepdims=True)
        s_hi = jnp.sum(jnp.where(lo, 0.0, zb), axis=1, keepdims=True)
        cols.append(jnp.where(lo, s_lo, s_hi))
    return cols[0] if len(cols) == 1 else jnp.concatenate(cols, axis=1)


def _silu(x):
    return x * jax.nn.sigmoid(x)


def _softplus(z):
    return jnp.maximum(z, 0.0) + jnp.log1p(jnp.exp(-jnp.abs(z)))


def _prenorm(x, gain, scale, shift):
    ms = jnp.mean(x * x, axis=-1, keepdims=True)
    y = x * lax.rsqrt(ms + RMS_EPS) * gain
    return y * (1.0 + scale) + shift


def _gated_residual(y, g, x, gate, w_out):
    a = (y * _silu(g)).astype(BF16)
    return x + gate * jnp.dot(a, w_out, preferred_element_type=F32)


def _mod_kernel(c_ref, w_ref, b_ref, o_ref):
    o_ref[0] = jnp.dot(_silu(c_ref[...]), w_ref[0], precision=HIGHEST,
                       preferred_element_type=F32) + b_ref[0]


def _modulation(c, mod_w, mod_b):
    depth, d, _ = mod_w.shape
    bsz = c.shape[0]
    return pl.pallas_call(
        _mod_kernel,
        out_shape=jax.ShapeDtypeStruct((depth, bsz, 3 * d), F32),
        grid=(depth, 3),
        in_specs=[pl.BlockSpec((bsz, d), lambda l, j: (0, 0)),
                  pl.BlockSpec((1, d, d), lambda l, j: (l, 0, j)),
                  pl.BlockSpec((1, 1, d), lambda l, j: (l, 0, j))],
        out_specs=pl.BlockSpec((1, bsz, d), lambda l, j: (l, 0, j)),
        compiler_params=_params("arbitrary", "arbitrary"),
        name="adaln_mod",
    )(c, mod_w, mod_b.reshape(depth, 1, 3 * d))


def _mod_specs(d):
    return [pl.BlockSpec((1, 1, d), lambda b, t, j=j: (b, 0, j)) for j in range(3)]


def _head_group_ones():
    r = jnp.arange(LANES) // DSA_HEAD_DIM
    return (r[:, None] == r[None, :]).astype(BF16)


def _lower_tri_ones(n):
    return (jnp.arange(n)[:, None] >= jnp.arange(n)[None, :]).astype(BF16)


def _rope_tables(seq, dim, reps):
    half = dim // 2
    inv_freq = ROPE_THETA ** (-jnp.arange(half, dtype=F32) / half)
    ang = jnp.arange(seq, dtype=F32)[:, None] * inv_freq[None, :]
    cos = jnp.concatenate([jnp.cos(ang), jnp.cos(ang)], axis=1)
    sin = jnp.concatenate([-jnp.sin(ang), jnp.sin(ang)], axis=1)
    return jnp.tile(cos, (1, reps)), jnp.tile(sin, (1, reps))


def _rope64(x, cos, sin, lane_lo):
    cols = []
    for c in range(x.shape[1] // LANES):
        xb = x[:, c * LANES:(c + 1) * LANES]
        rot = jnp.where(lane_lo, pltpu.roll(xb, 96, 1), pltpu.roll(xb, 32, 1))
        cols.append(xb * cos + rot * sin)
    return cols[0] if len(cols) == 1 else jnp.concatenate(cols, axis=1)


def _rope128(x, cos, sin):
    cols = []
    for c in range(x.shape[1] // LANES):
        xb = x[:, c * LANES:(c + 1) * LANES]
        cols.append(xb * cos + pltpu.roll(xb, 64, 1) * sin)
    return cols[0] if len(cols) == 1 else jnp.concatenate(cols, axis=1)


_DSA_Q = DSA_HEADS * DSA_HEAD_DIM
_DSA_KV2 = DSA_KV_HEADS * LANES
_DSA_QI = DSA_IDX_HEADS * DSA_IDX_DIM
_DSA_OFF_Q = 0
_DSA_OFF_G = _DSA_OFF_Q + _DSA_Q
_DSA_OFF_QI = _DSA_OFF_G + _DSA_Q
_DSA_OFF_K = _DSA_OFF_QI + _DSA_QI
_DSA_OFF_V = _DSA_OFF_K + _DSA_KV2
_DSA_OFF_KI = _DSA_OFF_V + _DSA_KV2
_DSA_OFF_WI = _DSA_OFF_KI + DSA_IDX_DIM
_DSA_COLS = _DSA_OFF_WI + LANES


def _dsa_proj_kernel(x_ref, gain_ref, shift_ref, scale_ref, w_ref, qgain_ref, kgain_ref,
                     cos64_ref, sin64_ref, cos128_ref, sin128_ref, g01_ref,
                     q_ref, g_ref, qi_ref, kt_ref, v_ref, ki_ref, wi_ref):
    h = _prenorm(x_ref[0], gain_ref[...], scale_ref[0], shift_ref[0]).astype(BF16)
    g01 = g01_ref[...]
    cos64, sin64 = cos64_ref[...], sin64_ref[...]
    cos128, sin128 = cos128_ref[...], sin128_ref[...]
    lane = lax.broadcasted_iota(jnp.int32, cos64.shape, 1)
    lane_lo = (lane % DSA_HEAD_DIM) < DSA_HEAD_DIM // 2
    lane_lo64 = lane < DSA_HEAD_DIM

    def head_norm_rope(raw, gain):
        ms = _group_sum(raw * raw, g01) * (1.0 / DSA_HEAD_DIM)
        return _rope64(raw * lax.rsqrt(ms + RMS_EPS) * gain, cos64, sin64, lane_lo)

    def col(off, n):
        return jnp.dot(h, w_ref[:, off:off + n], preferred_element_type=F32)

    q = head_norm_rope(col(_DSA_OFF_Q, _DSA_Q), qgain_ref[...]) * (LOG2E * DSA_HEAD_DIM ** -0.5)
    for hh in range(DSA_HEADS):
        qb = q[:, (hh // 2) * LANES:(hh // 2 + 1) * LANES]
        keep = lane_lo64 if hh % 2 == 0 else ~lane_lo64
        q_ref[0, hh] = jnp.where(keep, qb, 0.0).astype(BF16)
    g_ref[0] = col(_DSA_OFF_G, _DSA_Q).astype(BF16)
    qi = _rope128(col(_DSA_OFF_QI, _DSA_QI), cos128, sin128).astype(BF16)
    for hh in range(DSA_IDX_HEADS):
        qi_ref[0, hh] = qi[:, hh * DSA_IDX_DIM:(hh + 1) * DSA_IDX_DIM]
    k2 = head_norm_rope(col(_DSA_OFF_K, _DSA_KV2), kgain_ref[...])
    kt = k2.T.astype(BF16)
    for kv in range(DSA_KV_HEADS):
        kt_ref[0, kv] = kt[kv * LANES:(kv + 1) * LANES, :]
    v2 = col(_DSA_OFF_V, _DSA_KV2)
    for kv in range(DSA_KV_HEADS):
        v_ref[0, kv] = jnp.where(lane_lo64, v2[:, kv * LANES:(kv + 1) * LANES], 1.0).astype(BF16)
    ki = _rope128(col(_DSA_OFF_KI, DSA_IDX_DIM), cos128, sin128)
    ki_ref[0] = ki.T.astype(BF16)
    wi_ref[0] = col(_DSA_OFF_WI, LANES) * DSA_IDX_SCALE


def _dsa_weights(w_in):
    d = w_in.shape[0]
    q_end = _DSA_Q
    k_end = q_end + DSA_KV_HEADS * DSA_HEAD_DIM
    v_end = k_end + DSA_KV_HEADS * DSA_HEAD_DIM
    g_end = v_end + _DSA_Q
    qi_end = g_end + _DSA_QI
    wi_end = qi_end + DSA_IDX_HEADS
    wq, wk, wv, wg = w_in[:, :q_end], w_in[:, q_end:k_end], w_in[:, k_end:v_end], w_in[:, v_end:g_end]
    wqi, wwi, wki = w_in[:, g_end:qi_end], w_in[:, qi_end:wi_end], w_in[:, wi_end:]

    def dup(w):
        w = w.reshape(d, DSA_KV_HEADS, 1, DSA_HEAD_DIM)
        return jnp.broadcast_to(w, (d, DSA_KV_HEADS, 2, DSA_HEAD_DIM)).reshape(d, _DSA_KV2)

    wwi = jnp.pad(wwi, ((0, 0), (0, LANES - DSA_IDX_HEADS)))
    return jnp.concatenate([wq, wg, wqi, dup(wk), dup(wv), wki, wwi], axis=1).astype(BF16)


def _dsa_project(x, gain, mod3, w_in, q_gain, k_gain, tm=512):
    bsz, seq, d = x.shape
    tm = min(tm, seq)
    w = _dsa_weights(w_in)
    cos64, sin64 = _rope_tables(seq, DSA_HEAD_DIM, LANES // DSA_HEAD_DIM)
    cos128, sin128 = _rope_tables(seq, DSA_IDX_DIM, 1)
    shift_spec, scale_spec, _ = _mod_specs(d)
    table = pl.BlockSpec((tm, LANES), lambda b, t: (t, 0))
    row = lambda n, dt: jax.ShapeDtypeStruct((bsz, seq, n), dt)
    heads = lambda n: jax.ShapeDtypeStruct((bsz, n, seq, LANES), BF16)
    head_spec = lambda n: pl.BlockSpec((1, n, tm, LANES), lambda b, t: (b, 0, t, 0))
    return pl.pallas_call(
        _dsa_proj_kernel,
        out_shape=[heads(DSA_HEADS), row(_DSA_Q, BF16), heads(DSA_IDX_HEADS),
                   jax.ShapeDtypeStruct((bsz, DSA_KV_HEADS, LANES, seq), BF16),
                   heads(DSA_KV_HEADS),
                   jax.ShapeDtypeStruct((bsz, DSA_IDX_DIM, seq), BF16), row(LANES, F32)],
        grid=(bsz, seq // tm),
        in_specs=[pl.BlockSpec((1, tm, d), lambda b, t: (b, t, 0)),
                  _resident((1, d)), shift_spec, scale_spec, _resident(w.shape),
                  _resident((1, _DSA_Q)), _resident((1, _DSA_KV2)),
                  table, table, table, table, _resident((LANES, LANES))],
        out_specs=[head_spec(DSA_HEADS),
                   pl.BlockSpec((1, tm, _DSA_Q), lambda b, t: (b, t, 0)),
                   head_spec(DSA_IDX_HEADS),
                   pl.BlockSpec((1, DSA_KV_HEADS, LANES, tm), lambda b, t: (b, 0, 0, t)),
                   head_spec(DSA_KV_HEADS),
                   pl.BlockSpec((1, DSA_IDX_DIM, tm), lambda b, t: (b, 0, t)),
                   pl.BlockSpec((1, tm, LANES), lambda b, t: (b, t, 0))],
        compiler_params=_params("parallel", "parallel"),
        name="dsa_proj",
    )(x, gain.reshape(1, d), mod3, mod3, w,
      jnp.tile(q_gain, DSA_HEADS).reshape(1, _DSA_Q),
      jnp.tile(k_gain, _DSA_KV2 // DSA_HEAD_DIM).reshape(1, _DSA_KV2),
      cos64, sin64, cos128, sin128, _head_group_ones())


def _sortable_to_float(key):
    bits = jnp.where(key >= 0, key, key ^ jnp.int32(0x7FFFFFFF))
    return lax.bitcast_convert_type(bits, F32)


def _dsa_attn_kernel(n_sel, chunk, qi_ref, wi_ref, ki_ref, q_ref, kt_ref, v_ref, tri_ref,
                     g_ref, x_ref, gate_ref, wout_ref, o_ref, score_ref, bias_ref):
    qb, seq = score_ref.shape
    per = chunk // qb
    needed = lax.div(pl.program_id(1) + per, per)
    for j in range(seq // chunk):
        pl.when(needed == j + 1)(functools.partial(
            _dsa_attn_block, n_sel, chunk * (j + 1), qi_ref, wi_ref, ki_ref, q_ref, kt_ref, v_ref,
            tri_ref, g_ref, x_ref, gate_ref, wout_ref, o_ref, score_ref, bias_ref))


def _dsa_attn_block(n_sel, width, qi_ref, wi_ref, ki_ref, q_ref, kt_ref, v_ref, tri_ref,
                    g_ref, x_ref, gate_ref, wout_ref, o_ref, score_ref, bias_ref):
    blk = pl.program_id(1)
    qb = score_ref.shape[0]
    seq = width
    kit = ki_ref[0, :, 0:width]
    wi = wi_ref[0]
    stacked = DSA_HEADS // DSA_KV_HEADS

    acc = None
    for grp in range(DSA_IDX_HEADS // stacked):
        qs = qi_ref[0, grp * stacked:(grp + 1) * stacked].reshape(stacked * qb, DSA_IDX_DIM)
        logits = jnp.dot(qs, kit, preferred_element_type=F32)
        for j in range(stacked):
            hh = grp * stacked + j
            term = wi[:, hh:hh + 1] * jnp.maximum(logits[j * qb:(j + 1) * qb], 0.0)
            acc = term if acc is None else acc + term
    q_pos = blk * qb + lax.broadcasted_iota(jnp.int32, (qb, 1), 0)
    key_pos = lax.broadcasted_iota(jnp.int32, (1, seq), 1)
    score_ref[:, 0:width] = jnp.where(key_pos <= q_pos, acc, -jnp.inf)

    k_sel = jnp.float32(n_sel)

    def count_ge(thr):
        return jnp.sum(jnp.where(score_ref[:, 0:width] >= thr, 1.0, 0.0), axis=1, keepdims=True)

    def reaches(cand):
        return jnp.where(count_ge(_sortable_to_float(cand)) >= k_sel, 1, 0)

    int_min = jnp.int32(-2 ** 31)
    key0 = jnp.where(count_ge(jnp.zeros((qb, 1), F32)) >= k_sel, jnp.int32(0), int_min)

    def two_bits(i, key):
        unit = lax.shift_left(jnp.int32(1), jnp.int32(29) - 2 * i)
        taken = reaches(key + unit) + reaches(key + 2 * unit) + reaches(key + 3 * unit)
        return key + taken * unit

    key = lax.fori_loop(0, 15, two_bits, key0)
    thr = _sortable_to_float(key + reaches(key + 1))

    score = score_ref[:, 0:width]
    gt = score > thr
    need = k_sel - jnp.sum(jnp.where(gt, 1.0, 0.0), axis=1, keepdims=True)
    take_all = q_pos < n_sel
    tri = tri_ref[...]
    run = jnp.zeros((qb, 1), F32)
    for c in range(seq // LANES):
        sl = slice(c * LANES, (c + 1) * LANES)
        eq = jnp.where(score[:, sl] == thr, 1.0, 0.0)
        incl = jnp.dot(eq.astype(BF16), tri, preferred_element_type=F32)
        tie_ok = (incl - eq + run) < need
        run = run + incl[:, LANES - 1:LANES]
        sel = gt[:, sl] | ((eq > 0.0) & tie_ok) | take_all
        causal = key_pos[:, sl] <= q_pos
        bias_ref[:, sl] = jnp.where(sel & causal, 0.0, NEG_BIG)

    lane = lax.broadcasted_iota(jnp.int32, (qb, LANES), 1)
    lo = lane < DSA_HEAD_DIM
    bias = bias_ref[:, 0:width]
    outs = []
    for kv in range(DSA_KV_HEADS):
        q4 = q_ref[0, kv * stacked:(kv + 1) * stacked].reshape(stacked * qb, LANES)
        s4 = jnp.dot(q4, kt_ref[0, kv, :, 0:width], preferred_element_type=F32)
        ps = []
        for j in range(stacked):
            s = s4[j * qb:(j + 1) * qb] + bias
            ps.append(jnp.exp2(s - jnp.max(s, axis=1, keepdims=True)).astype(BF16))
        ov = jnp.dot(jnp.concatenate(ps, axis=0), v_ref[0, kv, 0:width, :],
                     preferred_element_type=F32)
        ov = ov / pltpu.roll(ov, DSA_HEAD_DIM, 1)
        for c in range(stacked // 2):
            even = ov[(2 * c) * qb:(2 * c + 1) * qb]
            odd = ov[(2 * c + 1) * qb:(2 * c + 2) * qb]
            outs.append(jnp.where(lo, even, pltpu.roll(odd, DSA_HEAD_DIM, 1)))
    o_ref[0] = _gated_residual(jnp.concatenate(outs, axis=1), g_ref[0].astype(F32), x_ref[0],
                               gate_ref[0], wout_ref[...])


def _dsa_attention(q, qi, kt2, v2, ki, wi, g, x, mod3, w_out):
    bsz, _, seq, _ = q.shape
    d = x.shape[2]
    qb = DSA_QBLOCK
    n_sel = min(DSA_TOPK, seq // 4)
    chunk = min(DSA_KEY_CHUNK, seq // 2)
    tri = (jnp.arange(LANES)[:, None] <= jnp.arange(LANES)[None, :]).astype(BF16)
    return pl.pallas_call(
        functools.partial(_dsa_attn_kernel, n_sel, chunk),
        out_shape=jax.ShapeDtypeStruct((bsz, seq, d), F32),
        grid=(bsz, seq // qb),
        in_specs=[pl.BlockSpec((1, DSA_IDX_HEADS, qb, LANES), lambda b, i: (b, 0, i, 0)),
                  pl.BlockSpec((1, qb, LANES), lambda b, i: (b, i, 0)),
                  pl.BlockSpec((1, DSA_IDX_DIM, seq), lambda b, i: (b, 0, 0)),
                  pl.BlockSpec((1, DSA_HEADS, qb, LANES), lambda b, i: (b, 0, i, 0)),
                  pl.BlockSpec((1, DSA_KV_HEADS, LANES, seq), lambda b, i: (b, 0, 0, 0)),
                  pl.BlockSpec((1, DSA_KV_HEADS, seq, LANES), lambda b, i: (b, 0, 0, 0)),
                  _resident((LANES, LANES)),
                  pl.BlockSpec((1, qb, _DSA_Q), lambda b, i: (b, i, 0)),
                  pl.BlockSpec((1, qb, d), lambda b, i: (b, i, 0)),
                  _mod_specs(d)[2],
                  _resident((_DSA_Q, d))],
        out_specs=pl.BlockSpec((1, qb, d), lambda b, i: (b, i, 0)),
        scratch_shapes=[pltpu.VMEM((qb, seq), F32), pltpu.VMEM((qb, seq), F32)],
        compiler_params=_params("parallel", "parallel"),
        name="dsa_attn",
    )(qi, wi, ki, q, kt2, v2, tri, g, x, mod3, w_out.astype(BF16))


def _dsa_layer(x, gain, mod3, w_in, q_gain, k_gain, w_out):
    q, g, qi, kt2, v2, ki, wi = _dsa_project(x, gain, mod3, w_in, q_gain, k_gain)
    return _dsa_attention(q, qi, kt2, v2, ki, wi, g, x, mod3, w_out)


def _lru_kernel(x_ref, gain_ref, shift_ref, scale_ref, gate_ref, win_ref, cw_ref, cb_ref,
                wa_ref, ba_ref, wx_ref, bx_ref, lam_ref, wout_ref, o_ref,
                h_buf, ubuf, a_buf, b_buf, hs_buf, h_carry):
    tt = x_ref.shape[1]
    width = ubuf.shape[1]
    halo = SUBLANES

    @pl.when(pl.program_id(1) == 0)
    def _():
        ubuf[0:halo, :] = jnp.zeros((halo, width), F32)
        h_carry[...] = jnp.zeros_like(h_carry)

    h_buf[...] = _prenorm(x_ref[0], gain_ref[...], scale_ref[0], shift_ref[0]).astype(BF16)
    ubuf[halo:halo + tt, :] = jnp.dot(h_buf[...], win_ref[:, 0:width], preferred_element_type=F32)
    cw = cw_ref[...]
    u = cb_ref[...]
    for j in range(LRU_CONV):
        start = halo - (LRU_CONV - 1) + j
        u = u + cw[j:j + 1, :] * ubuf[start:start + tt, :]
    ubuf[0:halo, :] = ubuf[tt:tt + halo, :]

    sp = _softplus(-lam_ref[...])
    for c in range(width // LRU_GROUP):
        sl = slice(c * LRU_GROUP, (c + 1) * LRU_GROUP)
        uc = u[:, sl]
        ub = uc.astype(BF16)
        r = jax.nn.sigmoid(jnp.dot(ub, wa_ref[c], preferred_element_type=F32) + ba_ref[:, sl])
        i = jax.nn.sigmoid(jnp.dot(ub, wx_ref[c], preferred_element_type=F32) + bx_ref[:, sl])
        a = jnp.exp(-LRU_C * r * sp[:, sl])
        a_buf[:, sl] = a
        b_buf[:, sl] = jnp.sqrt(1.0 - a * a) * (i * uc)

    row = lax.broadcasted_iota(jnp.int32, (SUBLANES, width), 0)

    def group(gi, h_prev):
        r0 = pl.multiple_of(gi * SUBLANES, SUBLANES)
        a = a_buf[pl.ds(r0, SUBLANES), :]
        b = b_buf[pl.ds(r0, SUBLANES), :]
        for s in (1, 2, 4):
            ok = row >= s
            b = jnp.where(ok, a * pltpu.roll(b, s, 0) + b, b)
            a = jnp.where(ok, a * pltpu.roll(a, s, 0), a)
        h = a * h_prev + b
        hs_buf[pl.ds(r0, SUBLANES), :] = h
        return jnp.broadcast_to(h[SUBLANES - 1:SUBLANES, :], (SUBLANES, width))

    h_carry[...] = lax.fori_loop(0, tt // SUBLANES, group, h_carry[...])
    g = jnp.dot(h_buf[...], win_ref[:, width:2 * width], preferred_element_type=F32)
    o_ref[0] = _gated_residual(hs_buf[...], g, x_ref[0], gate_ref[0], wout_ref[...])


def _lru_gate_blocks(w):
    nb, bd, _ = w.shape
    per = LRU_GROUP // bd
    w = w.reshape(nb // per, per, bd, bd)
    eye = jnp.eye(per, dtype=w.dtype)
    return jnp.einsum('gpcd,pq->gpcqd', w, eye).reshape(nb // per, LRU_GROUP, LRU_GROUP).astype(BF16)


def _lru_layer(x, gain, mod3, w_in, conv_w, conv_b, gate_a_w, gate_a_b, gate_x_w, gate_x_b, lam,
               w_out, tt=512):
    bsz, seq, d = x.shape
    tt = min(tt, seq)
    width = w_in.shape[1] // 2
    ngroups = width // LRU_GROUP
    tile = pl.BlockSpec((1, tt, d), lambda b, t: (b, t, 0))
    vec = lambda: _resident((1, width))
    gatew = lambda: _resident((ngroups, LRU_GROUP, LRU_GROUP))
    cw = jnp.pad(conv_w, ((0, SUBLANES - LRU_CONV), (0, 0)))
    return pl.pallas_call(
        _lru_kernel,
        out_shape=jax.ShapeDtypeStruct((bsz, seq, d), F32),
        grid=(bsz, seq // tt),
        in_specs=[tile, _resident((1, d)), *_mod_specs(d), _resident((d, 2 * width)),
                  _resident((SUBLANES, width)), vec(), gatew(), vec(), gatew(), vec(), vec(),
                  _resident((width, d))],
        out_specs=tile,
        scratch_shapes=[pltpu.VMEM((tt, d), BF16),
                        pltpu.VMEM((tt + SUBLANES, width), F32),
                        pltpu.VMEM((tt, width), F32), pltpu.VMEM((tt, width), F32),
                        pltpu.VMEM((tt, width), F32), pltpu.VMEM((SUBLANES, width), F32)],
        compiler_params=_params("parallel", "arbitrary"),
        name="rglru_layer",
    )(x, gain.reshape(1, d), mod3, mod3, mod3, w_in.astype(BF16), cw, conv_b.reshape(1, width),
      _lru_gate_blocks(gate_a_w), gate_a_b.reshape(1, width),
      _lru_gate_blocks(gate_x_w), gate_x_b.reshape(1, width), lam.reshape(1, width),
      w_out.astype(BF16))


def _rwkv_kernel(x_ref, gain_ref, shift_ref, scale_ref, gate_ref, mu_ref, w_ref, w1_ref, w2_ref,
                 a1_ref, a2_ref, w0_ref, a0_ref, kkw_ref, ka_ref, rk_ref, lnw_ref, lnb_ref,
                 ltri_ref, wout_ref, o_ref,
                 r_s, lw_s, k_s, v_s, kk_s, a_s, bonus_s, g_s, y_s, carry, state):
    tm = x_ref.shape[1]
    d = x_ref.shape[2]
    cs = RWKV_CHUNK
    hd = RWKV_HEAD_DIM

    @pl.when(pl.program_id(1) == 0)
    def _():
        carry[...] = jnp.zeros_like(carry)
        state[...] = jnp.zeros_like(state)

    lo_t = lax.broadcasted_iota(jnp.int32, (tm, LANES), 1) < hd
    h = _prenorm(x_ref[0], gain_ref[...], scale_ref[0], shift_ref[0])
    first = lax.broadcasted_iota(jnp.int32, (tm, d), 0) == 0
    h_prev = jnp.where(first, carry[0:1, :], pltpu.roll(h, 1, 0))
    carry[...] = jnp.broadcast_to(h[tm - 1:tm, :], carry.shape)
    delta = h_prev - h
    mu = mu_ref[...]

    def mix(n):
        return (h + delta * mu[n:n + 1, :]).astype(BF16)

    r = jnp.dot(mix(0), w_ref[0], preferred_element_type=F32)
    k = jnp.dot(mix(1), w_ref[1], preferred_element_type=F32)
    v = jnp.dot(mix(2), w_ref[2], preferred_element_type=F32)
    g_s[...] = jnp.dot(mix(3), w_ref[3], preferred_element_type=F32)
    w_lora = _dot(jnp.tanh(jnp.dot(mix(4), w1_ref[...], preferred_element_type=F32)), w2_ref[...])
    w_log = -_softplus(-(w0_ref[...] + w_lora)) - 0.5
    lw_s[...] = -jnp.exp(w_log)
    a_lora = _dot(jnp.dot(mix(5), a1_ref[...], preferred_element_type=F32), a2_ref[...])
    a = jax.nn.sigmoid(a0_ref[...] + a_lora)
    kk = k * kkw_ref[...]
    kk = kk / jnp.maximum(jnp.sqrt(_group_sum_lanes(kk * kk, lo_t)), 1e-12)
    k = k * (1.0 + (a - 1.0) * ka_ref[...])
    r_s[...] = r
    k_s[...] = k
    v_s[...] = v
    kk_s[...] = kk
    a_s[...] = a
    bonus_s[...] = _group_sum_lanes(r * k * rk_ref[...], lo_t) * v

    lane = lax.broadcasted_iota(jnp.int32, (cs, LANES), 1)
    lo = lane < hd
    ri = lax.broadcasted_iota(jnp.int32, (LANES, LANES), 0)
    ci = lax.broadcasted_iota(jnp.int32, (LANES, LANES), 1)
    same = (ri < cs) == (ci < cs)
    strict = same & (ci < ri)
    incl = same & (ci <= ri)
    eye = jnp.where(ri == ci, 1.0, 0.0)
    ltri = ltri_ref[...]

    def stack(z):
        return jnp.concatenate([z, z], axis=0)

    def split_heads(z):
        return jnp.concatenate([jnp.where(lo, z, 0.0), jnp.where(lo, 0.0, z)], axis=0)

    def own(z):
        return jnp.where(lo, z[0:cs], z[cs:2 * cs])

    pairs = range(d // LANES)
    lanes = [slice(p * LANES, (p + 1) * LANES) for p in pairs]

    def chunk(c, _):
        rows = pl.ds(pl.multiple_of(c * cs, cs), cs)
        lw = lw_s[rows, :]
        g_inc = _dot_exact_lhs(ltri, lw)
        g_last = g_inc[cs - 1:cs, :]
        kk = kk_s[rows, :]
        kc = k_s[rows, :]
        vc = v_s[rows, :]
        bvec = kk * a_s[rows, :]
        e_neg = jnp.exp(-g_inc)
        e_end = jnp.exp(g_last - g_inc)
        at = -kk * jnp.exp(g_inc - lw)
        rt = r_s[rows, :] * jnp.exp(g_inc)
        bt = bvec * e_neg
        kt = kc * e_neg
        bend = (bvec * e_end).astype(BF16)
        kend = (kc * e_end).astype(BF16)
        decay = jnp.exp(g_last)

        lhs = [jnp.concatenate([at[:, s], at[:, s], rt[:, s], rt[:, s]], axis=0).astype(BF16)
               for s in lanes]
        nb = [_dot_nt(lhs[p], split_heads(bt[:, lanes[p]])) for p in pairs]
        nk = [_dot_nt(lhs[p], split_heads(kt[:, lanes[p]])) for p in pairs]
        n_ab = [jnp.where(strict, nb[p][0:LANES], 0.0) for p in pairs]
        n_rb = [jnp.where(incl, nb[p][LANES:2 * LANES], 0.0).astype(BF16) for p in pairs]
        n_ak = [jnp.where(strict, nk[p][0:LANES], 0.0) for p in pairs]
        n_rk = [jnp.where(incl, nk[p][LANES:2 * LANES], 0.0) for p in pairs]

        inv = [eye + n_ab[p] for p in pairs]
        pw = [_dot(n_ab[p], n_ab[p]) for p in pairs]
        for _stage in range(4):
            res = [_dot(pw[p], jnp.concatenate([pw[p], inv[p]], axis=1)) for p in pairs]
            pw = [res[p][:, 0:LANES] for p in pairs]
            inv = [inv[p] + res[p][:, LANES:2 * LANES] for p in pairs]
        inv = [(inv[p] + _dot(pw[p], inv[p])).astype(BF16) for p in pairs]

        vv = [stack(vc[:, s]).astype(BF16) for s in lanes]
        w1 = [own(_dot(n_ak[p], vv[p])) for p in pairs]
        tz = [_dot(inv[p], jnp.concatenate([stack(at[:, lanes[p]]), stack(w1[p])], axis=1))
              for p in pairs]
        a2 = [own(tz[p][:, 0:LANES]) for p in pairs]
        u0 = [own(tz[p][:, LANES:2 * LANES]) for p in pairs]
        rz = [_dot(n_rb[p], jnp.concatenate([stack(a2[p]), stack(u0[p])], axis=1)) for p in pairs]
        r2 = [rt[:, lanes[p]] + own(rz[p][:, 0:LANES]) for p in pairs]
        y0 = [own(rz[p][:, LANES:2 * LANES]) + own(_dot(n_rk[p], vv[p])) for p in pairs]
        mlr = [jnp.where(same, _dot(a2[p].T, bend[:, lanes[p]]), 0.0) for p in pairs]
        c0 = [jnp.where(same, _dot(jnp.concatenate([u0[p], vc[:, lanes[p]]], axis=0).T,
                                   jnp.concatenate([bend[:, lanes[p]], kend[:, lanes[p]]], axis=0)), 0.0)
              for p in pairs]

        ys = []
        for p in pairs:
            s_bd = state[p]
            ys.append(_dot_nt(r2[p], s_bd) + y0[p])
            state[p] = s_bd * decay[:, lanes[p]] + _dot(s_bd, mlr[p]) + c0[p]
        y = jnp.concatenate(ys, axis=1)
        mean = _group_sum_lanes(y, lo) * (1.0 / hd)
        yc = y - mean
        var = _group_sum_lanes(yc * yc, lo) * (1.0 / hd)
        y_s[rows, :] = (yc * lax.rsqrt(var + RWKV_GN_EPS) * lnw_ref[...] + lnb_ref[...]
                        + bonus_s[rows, :])
        return 0

    lax.fori_loop(0, tm // cs, chunk, 0)
    o_ref[0] = _gated_residual(y_s[...], g_s[...], x_ref[0], gate_ref[0], wout_ref[...])


def _rwkv_layer(x, gain, mod3, mu, w_in, w0, w1, w2, a0, a1, a2, k_k, k_a, r_k, ln_w, ln_b, w_out,
                tm=512):
    bsz, seq, d = x.shape
    tm = min(tm, seq)
    rank = w1.shape[1]
    pad_c = lambda w: jnp.pad(w, ((0, 0), (0, LANES - rank))).astype(BF16)
    pad_r = lambda w: jnp.pad(w, ((0, LANES - rank), (0, 0))).astype(BF16)
    vec = lambda p: p.reshape(1, d)
    tile = pl.BlockSpec((1, tm, d), lambda b, t: (b, t, 0))
    buf = pltpu.VMEM((tm, d), F32)
    return pl.pallas_call(
        _rwkv_kernel,
        out_shape=jax.ShapeDtypeStruct((bsz, seq, d), F32),
        grid=(bsz, seq // tm),
        in_specs=[tile, _resident((1, d)), *_mod_specs(d), _resident((SUBLANES, d)),
                  _resident((4, d, d)), _resident((d, LANES)), _resident((LANES, d)),
                  _resident((d, LANES)), _resident((LANES, d))]
                 + [_resident((1, d))] * 7
                 + [_resident((RWKV_CHUNK, RWKV_CHUNK)), _resident((d, d))],
        out_specs=tile,
        scratch_shapes=[buf] * 9 + [pltpu.VMEM((SUBLANES, d), F32),
                                    pltpu.VMEM((d // LANES, LANES, LANES), F32)],
        compiler_params=_params("parallel", "arbitrary"),
        name="rwkv_layer",
    )(x, gain.reshape(1, d), mod3, mod3, mod3,
      jnp.pad(mu, ((0, SUBLANES - mu.shape[0]), (0, 0))),
      w_in.astype(BF16), pad_c(w1), pad_r(w2), pad_c(a1), pad_r(a2), vec(w0), vec(a0),
      vec(k_k), vec(k_a), vec(r_k), vec(ln_w), vec(ln_b),
      _lower_tri_ones(RWKV_CHUNK), w_out.astype(BF16))


def _gla_kernel(x_ref, gain_ref, shift_ref, scale_ref, gate_ref, win_ref, w2_ref, ab_ref,
                ngain_ref, ltri_ref, wout_ref, o_ref, o_buf, state):
    tt = x_ref.shape[1]
    key_dim = w2_ref.shape[1]
    val_dim = wout_ref.shape[0]
    dk = key_dim // GLA_HEADS
    dv = val_dim // GLA_HEADS
    cs = GLA_CHUNK

    @pl.when(pl.program_id(1) == 0)
    def _():
        state[...] = jnp.zeros_like(state)

    h = _prenorm(x_ref[0], gain_ref[...], scale_ref[0], shift_ref[0]).astype(BF16)

    def col(off, n):
        return jnp.dot(h, win_ref[:, off:off + n], preferred_element_type=F32)

    q = col(0, key_dim)
    k = col(key_dim, key_dim)
    v = col(2 * key_dim, val_dim)
    g = col(2 * key_dim + val_dim, val_dim)
    a_low = col(2 * key_dim + 2 * val_dim, LANES)

    ri = lax.broadcasted_iota(jnp.int32, (cs, cs), 0)
    ci = lax.broadcasted_iota(jnp.int32, (cs, cs), 1)
    causal = ci <= ri
    ltri = ltri_ref[...]
    z = _dot(a_low, w2_ref[...]) + ab_ref[...]
    log_alpha = -_softplus(-z) * (1.0 / GLA_GATE_NORM)

    for c in range(tt // cs):
        rows = slice(c * cs, (c + 1) * cs)
        cum = _dot_exact_lhs(ltri, log_alpha[rows, :])
        last = cum[cs - 1:cs, :]
        kc = k[rows, :]
        q_dec = q[rows, :] * (dk ** -0.5) * jnp.exp(cum)
        k_inv = kc * jnp.exp(-cum)
        k_end = kc * jnp.exp(last - cum)
        decay = jnp.exp(last)
        for hh in range(GLA_HEADS):
            ks = slice(hh * dk, (hh + 1) * dk)
            vs = slice(hh * dv, (hh + 1) * dv)
            vh = v[rows, vs]
            st = state[hh]
            att = jnp.where(causal, _dot_nt(q_dec[:, ks], k_inv[:, ks]), 0.0)
            o = _dot(att, vh) + _dot_nt(q_dec[:, ks], st)
            state[hh] = st * decay[:, ks] + _dot(vh.T, k_end[:, ks])
            ms = jnp.mean(o * o, axis=-1, keepdims=True)
            o_buf[rows, vs] = o * lax.rsqrt(ms + RMS_EPS) * ngain_ref[...]

    o_ref[0] = _gated_residual(o_buf[...], g, x_ref[0], gate_ref[0], wout_ref[...])


def _gla_layer(x, gain, mod3, w_in, alpha_w2, alpha_b, norm_gain, w_out, tt=512):
    bsz, seq, d = x.shape
    tt = min(tt, seq)
    key_dim, val_dim = alpha_w2.shape[1], w_out.shape[0]
    dk, dv = key_dim // GLA_HEADS, val_dim // GLA_HEADS
    w = jnp.pad(w_in, ((0, 0), (0, LANES - GLA_GATE_RANK))).astype(BF16)
    w2 = jnp.pad(alpha_w2, ((0, LANES - GLA_GATE_RANK), (0, 0))).astype(BF16)
    tile = pl.BlockSpec((1, tt, d), lambda b, t: (b, t, 0))
    return pl.pallas_call(
        _gla_kernel,
        out_shape=jax.ShapeDtypeStruct((bsz, seq, d), F32),
        grid=(bsz, seq // tt),
        in_specs=[tile, _resident((1, d)), *_mod_specs(d), _resident(w.shape),
                  _resident((LANES, key_dim)), _resident((1, key_dim)), _resident((1, dv)),
                  _resident((GLA_CHUNK, GLA_CHUNK)), _resident((val_dim, d))],
        out_specs=tile,
        scratch_shapes=[pltpu.VMEM((tt, val_dim), F32), pltpu.VMEM((GLA_HEADS, dv, dk), F32)],
        compiler_params=_params("parallel", "arbitrary"),
        name="gla_layer",
    )(x, gain.reshape(1, d), mod3, mod3, mod3, w, w2, alpha_b.reshape(1, key_dim),
      norm_gain.reshape(1, dv), _lower_tri_ones(GLA_CHUNK), w_out.astype(BF16))


def kernel(x, c, ln_gain, mod_w, mod_b, dsa_w_in, dsa_q_gain, dsa_k_gain, dsa_w_out, lru_w_in, lru_conv_w, lru_conv_b, lru_gate_a_w, lru_gate_a_b, lru_gate_x_w, lru_gate_x_b, lru_lambda, lru_w_out, rwkv_mu, rwkv_w_in, rwkv_w0, rwkv_w1, rwkv_w2, rwkv_a0, rwkv_a1, rwkv_a2, rwkv_k_k, rwkv_k_a, rwkv_r_k, rwkv_ln_w, rwkv_ln_b, rwkv_w_out, gla_w_in, gla_alpha_w2, gla_alpha_b, gla_norm_gain, gla_w_out):
    depth = mod_w.shape[0]
    bsz, _, d = x.shape
    mod = _modulation(c, mod_w, mod_b)
    for layer in range(depth):
        mixer, r = layer % 4, layer // 4
        mod3 = mod[layer].reshape(bsz, 1, 3 * d)
        gain = ln_gain[layer]
        if mixer == 0:
            x = _dsa_layer(x, gain, mod3, dsa_w_in[r], dsa_q_gain[r], dsa_k_gain[r], dsa_w_out[r])
        elif mixer == 1:
            x = _lru_layer(x, gain, mod3, lru_w_in[r], lru_conv_w[r], lru_conv_b[r], lru_gate_a_w[r],
                           lru_gate_a_b[r], lru_gate_x_w[r], lru_gate_x_b[r], lru_lambda[r], lru_w_out[r])
        elif mixer == 2:
            x = _rwkv_layer(x, gain, mod3, rwkv_mu[r], rwkv_w_in[r], rwkv_w0[r], rwkv_w1[r], rwkv_w2[r],
                            rwkv_a0[r], rwkv_a1[r], rwkv_a2[r], rwkv_k_k[r], rwkv_k_a[r],
                            rwkv_r_k[r].reshape(-1), rwkv_ln_w[r], rwkv_ln_b[r], rwkv_w_out[r])
        else:
            x = _gla_layer(x, gain, mod3, gla_w_in[r], gla_alpha_w2[r], gla_alpha_b[r],
                           gla_norm_gain[r], gla_w_out[r])
    return x
```

```python
import functools

import jax
import jax.numpy as jnp
from jax import lax
from jax.experimental import pallas as pl
from jax.experimental.pallas import tpu as pltpu

F32 = jnp.float32
BF16 = jnp.bfloat16
HIGHEST = lax.Precision.HIGHEST

LANES = 128
SUBLANES = 8
VMEM_LIMIT_BYTES = 56 * 1024 * 1024

RMS_EPS = 1e-6
ROPE_THETA = 10000.0

DSA_HEADS = 16
DSA_KV_HEADS = 4
DSA_HEAD_DIM = 64
DSA_IDX_HEADS = 8
DSA_IDX_DIM = 128
DSA_TOPK = 256
DSA_QBLOCK = 128
DSA_KEY_CHUNK = 512
LOG2E = 1.4426950408889634
DSA_IDX_SCALE = (DSA_IDX_HEADS * DSA_IDX_DIM) ** -0.5

LRU_BLOCKS = 16
LRU_CONV = 4
LRU_C = 8.0
LRU_GROUP = 256

RWKV_HEAD_DIM = 64
RWKV_GN_EPS = 64e-5
RWKV_CHUNK = 64

GLA_HEADS = 4
GLA_GATE_RANK = 16
GLA_GATE_NORM = 16.0
GLA_CHUNK = 64

NEG_BIG = -1e30
NT_DIMS = (((1,), (1,)), ((), ()))


def _params(*semantics):
    return pltpu.CompilerParams(dimension_semantics=semantics,
                                vmem_limit_bytes=VMEM_LIMIT_BYTES)


def _resident(shape):
    return pl.BlockSpec(shape, lambda *_: (0,) * len(shape), pipeline_mode=pl.Buffered(1))


def _dot(a, b):
    return jnp.dot(a.astype(BF16), b.astype(BF16), preferred_element_type=F32)


def _dot_nt(a, b):
    return lax.dot_general(a.astype(BF16), b.astype(BF16), NT_DIMS,
                           preferred_element_type=F32)


def _split3(x):
    hi = x.astype(BF16)
    r1 = x - hi.astype(F32)
    mid = r1.astype(BF16)
    lo = (r1 - mid.astype(F32)).astype(BF16)
    return hi, mid, lo


def _dot_exact_lhs(m01, x):
    hi, mid, lo = _split3(x)
    return (jnp.dot(m01, hi, preferred_element_type=F32)
            + jnp.dot(m01, mid, preferred_element_type=F32)
            + jnp.dot(m01, lo, preferred_element_type=F32))


def _dot_exact_rhs(x, m01):
    hi, mid, lo = _split3(x)
    return (jnp.dot(hi, m01, preferred_element_type=F32)
            + jnp.dot(mid, m01, preferred_element_type=F32)
            + jnp.dot(lo, m01, preferred_element_type=F32))


def _group_sum(z, g01):
    cols = [_dot_exact_rhs(z[:, c * LANES:(c + 1) * LANES], g01)
            for c in range(z.shape[1] // LANES)]
    return cols[0] if len(cols) == 1 else jnp.concatenate(cols, axis=1)


def _group_sum_lanes(z, lo):
    cols = []
    for c in range(z.shape[1] // LANES):
        zb = z[:, c * LANES:(c + 1) * LANES]
        s_lo = jnp.sum(jnp.where(lo, zb, 0.0), axis=1, keepdims=True)
        s_hi = jnp.sum(jnp.where(lo, 0.0, zb), axis=1, keepdims=True)
        cols.append(jnp.where(lo, s_lo, s_hi))
    return cols[0] if len(cols) == 1 else jnp.concatenate(cols, axis=1)


def _silu(x):
    return x * jax.nn.sigmoid(x)


def _softplus(z):
    return jnp.maximum(z, 0.0) + jnp.log1p(jnp.exp(-jnp.abs(z)))


def _prenorm(x, gain, scale, shift):
    ms = jnp.mean(x * x, axis=-1, keepdims=True)
    y = x * lax.rsqrt(ms + RMS_EPS) * gain
    return y * (1.0 + scale) + shift


def _gated_residual(y, g, x, gate, w_out):
    a = (y * _silu(g)).astype(BF16)
    return x + gate * jnp.dot(a, w_out, preferred_element_type=F32)


def _mod_kernel(c_ref, w_ref, b_ref, o_ref):
    o_ref[0] = jnp.dot(_silu(c_ref[...]), w_ref[0], precision=HIGHEST,
                       preferred_element_type=F32) + b_ref[0]


def _modulation(c, mod_w, mod_b):
    depth, d, _ = mod_w.shape
    bsz = c.shape[0]
    return pl.pallas_call(
        _mod_kernel,
        out_shape=jax.ShapeDtypeStruct((depth, bsz, 3 * d), F32),
        grid=(depth, 3),
        in_specs=[pl.BlockSpec((bsz, d), lambda l, j: (0, 0)),
                  pl.BlockSpec((1, d, d), lambda l, j: (l, 0, j)),
                  pl.BlockSpec((1, 1, d), lambda l, j: (l, 0, j))],
        out_specs=pl.BlockSpec((1, bsz, d), lambda l, j: (l, 0, j)),
        compiler_params=_params("arbitrary", "arbitrary"),
        name="adaln_mod",
    )(c, mod_w, mod_b.reshape(depth, 1, 3 * d))


def _mod_specs(d):
    return [pl.BlockSpec((1, 1, d), lambda b, t, j=j: (b, 0, j)) for j in range(3)]


def _head_group_ones():
    r = jnp.arange(LANES) // DSA_HEAD_DIM
    return (r[:, None] == r[None, :]).astype(BF16)


def _lower_tri_ones(n):
    return (jnp.arange(n)[:, None] >= jnp.arange(n)[None, :]).astype(BF16)


def _rope_tables(seq, dim, reps):
    half = dim // 2
    inv_freq = ROPE_THETA ** (-jnp.arange(half, dtype=F32) / half)
    ang = jnp.arange(seq, dtype=F32)[:, None] * inv_freq[None, :]
    cos = jnp.concatenate([jnp.cos(ang), jnp.cos(ang)], axis=1)
    sin = jnp.concatenate([-jnp.sin(ang), jnp.sin(ang)], axis=1)
    return jnp.tile(cos, (1, reps)), jnp.tile(sin, (1, reps))


def _rope64(x, cos, sin, lane_lo):
    cols = []
    for c in range(x.shape[1] // LANES):
        xb = x[:, c * LANES:(c + 1) * LANES]
        rot = jnp.where(lane_lo, pltpu.roll(xb, 96, 1), pltpu.roll(xb, 32, 1))
        cols.append(xb * cos + rot * sin)
    return cols[0] if len(cols) == 1 else jnp.concatenate(cols, axis=1)


def _rope128(x, cos, sin):
    cols = []
    for c in range(x.shape[1] // LANES):
        xb = x[:, c * LANES:(c + 1) * LANES]
        cols.append(xb * cos + pltpu.roll(xb, 64, 1) * sin)
    return cols[0] if len(cols) == 1 else jnp.concatenate(cols, axis=1)


_DSA_Q = DSA_HEADS * DSA_HEAD_DIM
_DSA_KV2 = DSA_KV_HEADS * LANES
_DSA_QI = DSA_IDX_HEADS * DSA_IDX_DIM
_DSA_OFF_Q = 0
_DSA_OFF_G = _DSA_OFF_Q + _DSA_Q
_DSA_OFF_QI = _DSA_OFF_G + _DSA_Q
_DSA_OFF_K = _DSA_OFF_QI + _DSA_QI
_DSA_OFF_V = _DSA_OFF_K + _DSA_KV2
_DSA_OFF_KI = _DSA_OFF_V + _DSA_KV2
_DSA_OFF_WI = _DSA_OFF_KI + DSA_IDX_DIM
_DSA_COLS = _DSA_OFF_WI + LANES


def _dsa_proj_kernel(x_ref, gain_ref, shift_ref, scale_ref, w_ref, qgain_ref, kgain_ref,
                     cos64_ref, sin64_ref, cos128_ref, sin128_ref, g01_ref,
                     q_ref, g_ref, qi_ref, k_ref, v_ref, ki_ref, wi_ref):
    h = _prenorm(x_ref[0], gain_ref[...], scale_ref[0], shift_ref[0]).astype(BF16)
    g01 = g01_ref[...]
    cos64, sin64 = cos64_ref[...], sin64_ref[...]
    cos128, sin128 = cos128_ref[...], sin128_ref[...]
    lane = lax.broadcasted_iota(jnp.int32, cos64.shape, 1)
    lane_lo = (lane % DSA_HEAD_DIM) < DSA_HEAD_DIM // 2

    def head_norm_rope(raw, gain):
        ms = _group_sum(raw * raw, g01) * (1.0 / DSA_HEAD_DIM)
        return _rope64(raw * lax.rsqrt(ms + RMS_EPS) * gain, cos64, sin64, lane_lo)

    def col(off, n):
        return jnp.dot(h, w_ref[:, off:off + n], preferred_element_type=F32)

    tm = h.shape[0]
    row_lo64 = lax.broadcasted_iota(jnp.int32, (LANES, tm), 0) < DSA_HEAD_DIM
    q = head_norm_rope(col(_DSA_OFF_Q, _DSA_Q), qgain_ref[...]) * (LOG2E * DSA_HEAD_DIM ** -0.5)
    qt = q.T
    for hh in range(DSA_HEADS):
        blk = qt[(hh // 2) * LANES:(hh // 2 + 1) * LANES, :]
        keep = row_lo64 if hh % 2 == 0 else ~row_lo64
        q_ref[0, hh] = jnp.where(keep, blk, 0.0).astype(BF16)
    g_ref[0] = col(_DSA_OFF_G, _DSA_Q).astype(BF16)
    qit = _rope128(col(_DSA_OFF_QI, _DSA_QI), cos128, sin128).T.astype(BF16)
    for hh in range(DSA_IDX_HEADS):
        qi_ref[0, hh] = qit[hh * DSA_IDX_DIM:(hh + 1) * DSA_IDX_DIM, :]
    k2 = head_norm_rope(col(_DSA_OFF_K, _DSA_KV2), kgain_ref[...]).astype(BF16)
    for kv in range(DSA_KV_HEADS):
        k_ref[0, kv] = k2[:, kv * LANES:(kv + 1) * LANES]
    vt = col(_DSA_OFF_V, _DSA_KV2).T
    for kv in range(DSA_KV_HEADS):
        v_ref[0, kv] = jnp.where(row_lo64, vt[kv * LANES:(kv + 1) * LANES, :], 1.0).astype(BF16)
    ki_ref[0] = _rope128(col(_DSA_OFF_KI, DSA_IDX_DIM), cos128, sin128).astype(BF16)
    wit = (col(_DSA_OFF_WI, LANES) * DSA_IDX_SCALE).T
    wi_ref[0] = wit[0:DSA_IDX_HEADS, :]


def _dsa_weights(w_in):
    d = w_in.shape[0]
    q_end = _DSA_Q
    k_end = q_end + DSA_KV_HEADS * DSA_HEAD_DIM
    v_end = k_end + DSA_KV_HEADS * DSA_HEAD_DIM
    g_end = v_end + _DSA_Q
    qi_end = g_end + _DSA_QI
    wi_end = qi_end + DSA_IDX_HEADS
    wq, wk, wv, wg = w_in[:, :q_end], w_in[:, q_end:k_end], w_in[:, k_end:v_end], w_in[:, v_end:g_end]
    wqi, wwi, wki = w_in[:, g_end:qi_end], w_in[:, qi_end:wi_end], w_in[:, wi_end:]

    def dup(w):
        w = w.reshape(d, DSA_KV_HEADS, 1, DSA_HEAD_DIM)
        return jnp.broadcast_to(w, (d, DSA_KV_HEADS, 2, DSA_HEAD_DIM)).reshape(d, _DSA_KV2)

    wwi = jnp.pad(wwi, ((0, 0), (0, LANES - DSA_IDX_HEADS)))
    return jnp.concatenate([wq, wg, wqi, dup(wk), dup(wv), wki, wwi], axis=1).astype(BF16)


def _dsa_project(x, gain, mod3, w_in, q_gain, k_gain, tm=512):
    bsz, seq, d = x.shape
    tm = min(tm, seq)
    w = _dsa_weights(w_in)
    cos64, sin64 = _rope_tables(seq, DSA_HEAD_DIM, LANES // DSA_HEAD_DIM)
    cos128, sin128 = _rope_tables(seq, DSA_IDX_DIM, 1)
    shift_spec, scale_spec, _ = _mod_specs(d)
    table = pl.BlockSpec((tm, LANES), lambda b, t: (t, 0))
    row = lambda n, dt: jax.ShapeDtypeStruct((bsz, seq, n), dt)
    heads = lambda n: jax.ShapeDtypeStruct((bsz, n, seq, LANES), BF16)
    head_spec = lambda n: pl.BlockSpec((1, n, tm, LANES), lambda b, t: (b, 0, t, 0))
    heads_t = lambda n: jax.ShapeDtypeStruct((bsz, n, LANES, seq), BF16)
    head_t_spec = lambda n: pl.BlockSpec((1, n, LANES, tm), lambda b, t: (b, 0, 0, t))
    return pl.pallas_call(
        _dsa_proj_kernel,
        out_shape=[heads_t(DSA_HEADS), row(_DSA_Q, BF16), heads_t(DSA_IDX_HEADS),
                   heads(DSA_KV_HEADS), heads_t(DSA_KV_HEADS), row(DSA_IDX_DIM, BF16),
                   jax.ShapeDtypeStruct((bsz, DSA_IDX_HEADS, seq), F32)],
        grid=(bsz, seq // tm),
        in_specs=[pl.BlockSpec((1, tm, d), lambda b, t: (b, t, 0)),
                  _resident((1, d)), shift_spec, scale_spec, _resident(w.shape),
                  _resident((1, _DSA_Q)), _resident((1, _DSA_KV2)),
                  table, table, table, table, _resident((LANES, LANES))],
        out_specs=[head_t_spec(DSA_HEADS),
                   pl.BlockSpec((1, tm, _DSA_Q), lambda b, t: (b, t, 0)),
                   head_t_spec(DSA_IDX_HEADS),
                   head_spec(DSA_KV_HEADS),
                   head_t_spec(DSA_KV_HEADS),
                   pl.BlockSpec((1, tm, DSA_IDX_DIM), lambda b, t: (b, t, 0)),
                   pl.BlockSpec((1, DSA_IDX_HEADS, tm), lambda b, t: (b, 0, t))],
        compiler_params=_params("parallel", "parallel"),
        name="dsa_proj",
    )(x, gain.reshape(1, d), mod3, mod3, w,
      jnp.tile(q_gain, DSA_HEADS).reshape(1, _DSA_Q),
      jnp.tile(k_gain, _DSA_KV2 // DSA_HEAD_DIM).reshape(1, _DSA_KV2),
      cos64, sin64, cos128, sin128, _head_group_ones())


def _reduce_rows(x, op):
    part = op(x.reshape(x.shape[0] // 64, 64, x.shape[1]), axis=0)
    return op(part, axis=0, keepdims=True)


def _sortable_to_float(key):
    bits = jnp.where(key >= 0, key, key ^ jnp.int32(0x7FFFFFFF))
    return lax.bitcast_convert_type(bits, F32)


def _dsa_attn_kernel(n_sel, chunk, qi_ref, wi_ref, ki_ref, q_ref, k_ref, v_ref, tri_ref,
                     g_ref, x_ref, gate_ref, wout_ref, o_ref, score_ref, bias_ref):
    seq, qb = score_ref.shape
    per = chunk // qb
    needed = lax.div(pl.program_id(1) + per, per)
    for j in range(seq // chunk):
        pl.when(needed == j + 1)(functools.partial(
            _dsa_attn_block, n_sel, chunk * (j + 1), qi_ref, wi_ref, ki_ref, q_ref, k_ref, v_ref,
            tri_ref, g_ref, x_ref, gate_ref, wout_ref, o_ref, score_ref, bias_ref))


def _dsa_attn_block(n_sel, width, qi_ref, wi_ref, ki_ref, q_ref, k_ref, v_ref, tri_ref,
                    g_ref, x_ref, gate_ref, wout_ref, o_ref, score_ref, bias_ref):
    blk = pl.program_id(1)
    qb = score_ref.shape[1]
    ki = ki_ref[0, 0:width, :]
    wi = wi_ref[0]
    stacked = DSA_HEADS // DSA_KV_HEADS

    def heads_of(ref, first):
        return jnp.concatenate([ref[0, first + j] for j in range(stacked)], axis=1)

    acc = None
    for grp in range(DSA_IDX_HEADS // stacked):
        logits = jnp.dot(ki, heads_of(qi_ref, grp * stacked), preferred_element_type=F32)
        for j in range(stacked):
            hh = grp * stacked + j
            term = wi[hh:hh + 1, :] * jnp.maximum(logits[:, j * qb:(j + 1) * qb], 0.0)
            acc = term if acc is None else acc + term
    q_pos = blk * qb + lax.broadcasted_iota(jnp.int32, (1, qb), 1)
    key_pos = lax.broadcasted_iota(jnp.int32, (width, 1), 0)
    score_ref[0:width, :] = jnp.where(key_pos <= q_pos, acc, -jnp.inf)

    k_sel = jnp.float32(n_sel)

    def count_ge(thr):
        return _reduce_rows(jnp.where(score_ref[0:width, :] >= thr, 1.0, 0.0), jnp.sum)

    def reaches(cand):
        return jnp.where(count_ge(_sortable_to_float(cand)) >= k_sel, 1, 0)

    int_min = jnp.int32(-2 ** 31)
    key0 = jnp.where(count_ge(jnp.zeros((1, qb), F32)) >= k_sel, jnp.int32(0), int_min)

    def one_bit(i, key):
        unit = lax.shift_left(jnp.int32(1), jnp.int32(30) - i)
        return key + reaches(key + unit) * unit

    thr = _sortable_to_float(lax.fori_loop(0, 31, one_bit, key0))

    score = score_ref[0:width, :]
    gt = score > thr
    need = k_sel - _reduce_rows(jnp.where(gt, 1.0, 0.0), jnp.sum)
    take_all = q_pos < n_sel
    tri = tri_ref[...]
    run = jnp.zeros((1, qb), F32)
    for c in range(width // LANES):
        sl = slice(c * LANES, (c + 1) * LANES)
        eq = jnp.where(score[sl, :] == thr, 1.0, 0.0)
        incl = jnp.dot(tri, eq.astype(BF16), preferred_element_type=F32)
        tie_ok = (incl - eq + run) < need
        run = run + incl[LANES - 1:LANES, :]
        sel = gt[sl, :] | ((eq > 0.0) & tie_ok) | take_all
        causal = key_pos[sl, :] <= q_pos
        bias_ref[sl, :] = jnp.where(sel & causal, 0.0, NEG_BIG)

    bias = bias_ref[0:width, :]
    outs = []
    for kv in range(DSA_KV_HEADS):
        s4 = jnp.dot(k_ref[0, kv, 0:width, :], heads_of(q_ref, kv * stacked),
                     preferred_element_type=F32)
        ps = []
        for j in range(stacked):
            s = s4[:, j * qb:(j + 1) * qb] + bias
            ps.append(jnp.exp2(s - _reduce_rows(s, jnp.max)).astype(BF16))
        ov = jnp.dot(v_ref[0, kv, :, 0:width], jnp.concatenate(ps, axis=1),
                     preferred_element_type=F32)
        ov = ov[0:DSA_HEAD_DIM, :] / ov[DSA_HEAD_DIM:DSA_HEAD_DIM + 1, :]
        for c in range(stacked // 2):
            even = ov[:, (2 * c) * qb:(2 * c + 1) * qb]
            odd = ov[:, (2 * c + 1) * qb:(2 * c + 2) * qb]
            outs.append(jnp.concatenate([even, odd], axis=0).T)
    o_ref[0] = _gated_residual(jnp.concatenate(outs, axis=1), g_ref[0].astype(F32), x_ref[0],
                               gate_ref[0], wout_ref[...])


def _dsa_attention(q, qi, k2, vt, ki, wi, g, x, mod3, w_out):
    bsz, seq, d = x.shape
    qb = DSA_QBLOCK
    n_sel = min(DSA_TOPK, seq // 4)
    chunk = min(DSA_KEY_CHUNK, seq // 2)
    return pl.pallas_call(
        functools.partial(_dsa_attn_kernel, n_sel, chunk),
        out_shape=jax.ShapeDtypeStruct((bsz, seq, d), F32),
        grid=(bsz, seq // qb),
        in_specs=[pl.BlockSpec((1, DSA_IDX_HEADS, LANES, qb), lambda b, i: (b, 0, 0, i)),
                  pl.BlockSpec((1, DSA_IDX_HEADS, qb), lambda b, i: (b, 0, i)),
                  pl.BlockSpec((1, seq, DSA_IDX_DIM), lambda b, i: (b, 0, 0)),
                  pl.BlockSpec((1, DSA_HEADS, LANES, qb), lambda b, i: (b, 0, 0, i)),
                  pl.BlockSpec((1, DSA_KV_HEADS, seq, LANES), lambda b, i: (b, 0, 0, 0)),
                  pl.BlockSpec((1, DSA_KV_HEADS, LANES, seq), lambda b, i: (b, 0, 0, 0)),
                  _resident((LANES, LANES)),
                  pl.BlockSpec((1, qb, _DSA_Q), lambda b, i: (b, i, 0)),
                  pl.BlockSpec((1, qb, d), lambda b, i: (b, i, 0)),
                  _mod_specs(d)[2],
                  _resident((_DSA_Q, d))],
        out_specs=pl.BlockSpec((1, qb, d), lambda b, i: (b, i, 0)),
        scratch_shapes=[pltpu.VMEM((seq, qb), F32), pltpu.VMEM((seq, qb), F32)],
        compiler_params=_params("parallel", "parallel"),
        name="dsa_attn",
    )(qi, wi, ki, q, k2, vt, _lower_tri_ones(LANES), g, x, mod3, w_out.astype(BF16))


def _dsa_layer(x, gain, mod3, w_in, q_gain, k_gain, w_out):
    q, g, qi, k2, vt, ki, wi = _dsa_project(x, gain, mod3, w_in, q_gain, k_gain)
    return _dsa_attention(q, qi, k2, vt, ki, wi, g, x, mod3, w_out)


def _lru_kernel(x_ref, gain_ref, shift_ref, scale_ref, gate_ref, win_ref, cw_ref, cb_ref,
                wa_ref, ba_ref, wx_ref, bx_ref, lam_ref, wout_ref, o_ref,
                h_buf, ubuf, a_buf, b_buf, hs_buf, h_carry):
    tt = x_ref.shape[1]
    width = ubuf.shape[1]
    halo = SUBLANES

    @pl.when(pl.program_id(1) == 0)
    def _():
        ubuf[0:halo, :] = jnp.zeros((halo, width), F32)
        h_carry[...] = jnp.zeros_like(h_carry)

    h_buf[...] = _prenorm(x_ref[0], gain_ref[...], scale_ref[0], shift_ref[0]).astype(BF16)
    ubuf[halo:halo + tt, :] = jnp.dot(h_buf[...], win_ref[:, 0:width], preferred_element_type=F32)
    cw = cw_ref[...]
    u = cb_ref[...]
    for j in range(LRU_CONV):
        start = halo - (LRU_CONV - 1) + j
        u = u + cw[j:j + 1, :] * ubuf[start:start + tt, :]
    ubuf[0:halo, :] = ubuf[tt:tt + halo, :]

    sp = _softplus(-lam_ref[...])
    for c in range(width // LRU_GROUP):
        sl = slice(c * LRU_GROUP, (c + 1) * LRU_GROUP)
        uc = u[:, sl]
        ub = uc.astype(BF16)
        r = jax.nn.sigmoid(jnp.dot(ub, wa_ref[c], preferred_element_type=F32) + ba_ref[:, sl])
        i = jax.nn.sigmoid(jnp.dot(ub, wx_ref[c], preferred_element_type=F32) + bx_ref[:, sl])
        a = jnp.exp(-LRU_C * r * sp[:, sl])
        a_buf[:, sl] = a
        b_buf[:, sl] = jnp.sqrt(1.0 - a * a) * (i * uc)

    row = lax.broadcasted_iota(jnp.int32, (SUBLANES, width), 0)

    def group(gi, h_prev):
        r0 = pl.multiple_of(gi * SUBLANES, SUBLANES)
        a = a_buf[pl.ds(r0, SUBLANES), :]
        b = b_buf[pl.ds(r0, SUBLANES), :]
        for s in (1, 2, 4):
            ok = row >= s
            b = jnp.where(ok, a * pltpu.roll(b, s, 0) + b, b)
            a = jnp.where(ok, a * pltpu.roll(a, s, 0), a)
        h = a * h_prev + b
        hs_buf[pl.ds(r0, SUBLANES), :] = h
        return jnp.broadcast_to(h[SUBLANES - 1:SUBLANES, :], (SUBLANES, width))

    h_carry[...] = lax.fori_loop(0, tt // SUBLANES, group, h_carry[...])
    g = jnp.dot(h_buf[...], win_ref[:, width:2 * width], preferred_element_type=F32)
    o_ref[0] = _gated_residual(hs_buf[...], g, x_ref[0], gate_ref[0], wout_ref[...])


def _lru_gate_blocks(w):
    nb, bd, _ = w.shape
    per = LRU_GROUP // bd
    w = w.reshape(nb // per, per, bd, bd)
    eye = jnp.eye(per, dtype=w.dtype)
    return jnp.einsum('gpcd,pq->gpcqd', w, eye).reshape(nb // per, LRU_GROUP, LRU_GROUP).astype(BF16)


def _lru_layer(x, gain, mod3, w_in, conv_w, conv_b, gate_a_w, gate_a_b, gate_x_w, gate_x_b, lam,
               w_out, tt=512):
    bsz, seq, d = x.shape
    tt = min(tt, seq)
    width = w_in.shape[1] // 2
    ngroups = width // LRU_GROUP
    tile = pl.BlockSpec((1, tt, d), lambda b, t: (b, t, 0))
    vec = lambda: _resident((1, width))
    gatew = lambda: _resident((ngroups, LRU_GROUP, LRU_GROUP))
    cw = jnp.pad(conv_w, ((0, SUBLANES - LRU_CONV), (0, 0)))
    return pl.pallas_call(
        _lru_kernel,
        out_shape=jax.ShapeDtypeStruct((bsz, seq, d), F32),
        grid=(bsz, seq // tt),
        in_specs=[tile, _resident((1, d)), *_mod_specs(d), _resident((d, 2 * width)),
                  _resident((SUBLANES, width)), vec(), gatew(), vec(), gatew(), vec(), vec(),
                  _resident((width, d))],
        out_specs=tile,
        scratch_shapes=[pltpu.VMEM((tt, d), BF16),
                        pltpu.VMEM((tt + SUBLANES, width), F32),
                        pltpu.VMEM((tt, width), F32), pltpu.VMEM((tt, width), F32),
                        pltpu.VMEM((tt, width), F32), pltpu.VMEM((SUBLANES, width), F32)],
        compiler_params=_params("parallel", "arbitrary"),
        name="rglru_layer",
    )(x, gain.reshape(1, d), mod3, mod3, mod3, w_in.astype(BF16), cw, conv_b.reshape(1, width),
      _lru_gate_blocks(gate_a_w), gate_a_b.reshape(1, width),
      _lru_gate_blocks(gate_x_w), gate_x_b.reshape(1, width), lam.reshape(1, width),
      w_out.astype(BF16))


def _rwkv_kernel(x_ref, gain_ref, shift_ref, scale_ref, gate_ref, mu_ref, w_ref, w1_ref, w2_ref,
                 a1_ref, a2_ref, w0_ref, a0_ref, kkw_ref, ka_ref, rk_ref, lnw_ref, lnb_ref,
                 ltri_ref, wout_ref, o_ref,
                 r_s, lw_s, k_s, v_s, kk_s, a_s, bonus_s, g_s, y_s, carry, state):
    tm = x_ref.shape[1]
    d = x_ref.shape[2]
    cs = RWKV_CHUNK
    hd = RWKV_HEAD_DIM

    @pl.when(pl.program_id(1) == 0)
    def _():
        carry[...] = jnp.zeros_like(carry)
        state[...] = jnp.zeros_like(state)

    lo_t = lax.broadcasted_iota(jnp.int32, (tm, LANES), 1) < hd
    h = _prenorm(x_ref[0], gain_ref[...], scale_ref[0], shift_ref[0])
    first = lax.broadcasted_iota(jnp.int32, (tm, d), 0) == 0
    h_prev = jnp.where(first, carry[0:1, :], pltpu.roll(h, 1, 0))
    carry[...] = jnp.broadcast_to(h[tm - 1:tm, :], carry.shape)
    delta = h_prev - h
    mu = mu_ref[...]

    def mix(n):
        return (h + delta * mu[n:n + 1, :]).astype(BF16)

    r = jnp.dot(mix(0), w_ref[0], preferred_element_type=F32)
    k = jnp.dot(mix(1), w_ref[1], preferred_element_type=F32)
    v = jnp.dot(mix(2), w_ref[2], preferred_element_type=F32)
    g_s[...] = jnp.dot(mix(3), w_ref[3], preferred_element_type=F32)
    w_lora = _dot(jnp.tanh(jnp.dot(mix(4), w1_ref[...], preferred_element_type=F32)), w2_ref[...])
    w_log = -_softplus(-(w0_ref[...] + w_lora)) - 0.5
    lw_s[...] = -jnp.exp(w_log)
    a_lora = _dot(jnp.dot(mix(5), a1_ref[...], preferred_element_type=F32), a2_ref[...])
    a = jax.nn.sigmoid(a0_ref[...] + a_lora)
    kk = k * kkw_ref[...]
    kk = kk / jnp.maximum(jnp.sqrt(_group_sum_lanes(kk * kk, lo_t)), 1e-12)
    k = k * (1.0 + (a - 1.0) * ka_ref[...])
    r_s[...] = r
    k_s[...] = k
    v_s[...] = v
    kk_s[...] = kk
    a_s[...] = a
    bonus_s[...] = _group_sum_lanes(r * k * rk_ref[...], lo_t) * v

    lane = lax.broadcasted_iota(jnp.int32, (cs, LANES), 1)
    lo = lane < hd
    ri = lax.broadcasted_iota(jnp.int32, (LANES, LANES), 0)
    ci = lax.broadcasted_iota(jnp.int32, (LANES, LANES), 1)
    same = (ri < cs) == (ci < cs)
    strict = same & (ci < ri)
    incl = same & (ci <= ri)
    eye = jnp.where(ri == ci, 1.0, 0.0)
    ltri = ltri_ref[...]

    def stack(z):
        return jnp.concatenate([z, z], axis=0)

    def split_heads(z):
        return jnp.concatenate([jnp.where(lo, z, 0.0), jnp.where(lo, 0.0, z)], axis=0)

    def own(z):
        return jnp.where(lo, z[0:cs], z[cs:2 * cs])

    pairs = range(d // LANES)
    lanes = [slice(p * LANES, (p + 1) * LANES) for p in pairs]

    def chunk(c, _):
        rows = pl.ds(pl.multiple_of(c * cs, cs), cs)
        lw = lw_s[rows, :]
        g_inc = _dot_exact_lhs(ltri, lw)
        g_last = g_inc[cs - 1:cs, :]
        kk = kk_s[rows, :]
        kc = k_s[rows, :]
        vc = v_s[rows, :]
        bvec = kk * a_s[rows, :]
        e_neg = jnp.exp(-g_inc)
        e_end = jnp.exp(g_last - g_inc)
        at = -kk * jnp.exp(g_inc - lw)
        rt = r_s[rows, :] * jnp.exp(g_inc)
        bt = bvec * e_neg
        kt = kc * e_neg
        bend = (bvec * e_end).astype(BF16)
        kend = (kc * e_end).astype(BF16)
        decay = jnp.exp(g_last)

        lhs = [jnp.concatenate([at[:, s], rt[:, s]], axis=0).astype(BF16) for s in lanes]
        nb = [_dot_nt(lhs[p], split_heads(bt[:, lanes[p]])) for p in pairs]
        nk = [_dot_nt(lhs[p], split_heads(kt[:, lanes[p]])) for p in pairs]
        n_ab = [jnp.where(strict, stack(nb[p][0:cs]), 0.0) for p in pairs]
        n_rb = [jnp.where(incl, stack(nb[p][cs:2 * cs]), 0.0).astype(BF16) for p in pairs]
        n_ak = [jnp.where(strict, stack(nk[p][0:cs]), 0.0) for p in pairs]
        n_rk = [jnp.where(incl, stack(nk[p][cs:2 * cs]), 0.0) for p in pairs]

        inv = [eye + n_ab[p] for p in pairs]
        pw = [_dot(n_ab[p], n_ab[p]) for p in pairs]
        for _stage in range(4):
            res = [_dot(pw[p], jnp.concatenate([pw[p], inv[p]], axis=1)) for p in pairs]
            pw = [res[p][:, 0:LANES] for p in pairs]
            inv = [inv[p] + res[p][:, LANES:2 * LANES] for p in pairs]
        inv = [(inv[p] + _dot(pw[p], inv[p])).astype(BF16) for p in pairs]

        vv = [stack(vc[:, s]).astype(BF16) for s in lanes]
        w1 = [own(_dot(n_ak[p], vv[p])) for p in pairs]
        tz = [_dot(inv[p], jnp.concatenate([stack(at[:, lanes[p]]), stack(w1[p])], axis=1))
              for p in pairs]
        a2 = [own(tz[p][:, 0:LANES]) for p in pairs]
        u0 = [own(tz[p][:, LANES:2 * LANES]) for p in pairs]
        rz = [_dot(n_rb[p], jnp.concatenate([stack(a2[p]), stack(u0[p])], axis=1)) for p in pairs]
        r2 = [rt[:, lanes[p]] + own(rz[p][:, 0:LANES]) for p in pairs]
        y0 = [own(rz[p][:, LANES:2 * LANES]) + own(_dot(n_rk[p], vv[p])) for p in pairs]
        mlr = [jnp.where(same, _dot(a2[p].T, bend[:, lanes[p]]), 0.0) for p in pairs]
        c0 = [jnp.where(same, _dot(jnp.concatenate([u0[p], vc[:, lanes[p]]], axis=0).T,
                                   jnp.concatenate([bend[:, lanes[p]], kend[:, lanes[p]]], axis=0)), 0.0)
              for p in pairs]

        ys = []
        for p in pairs:
            s_bd = state[p]
            ys.append(_dot_nt(r2[p], s_bd) + y0[p])
            state[p] = s_bd * decay[:, lanes[p]] + _dot(s_bd, mlr[p]) + c0[p]
        y = jnp.concatenate(ys, axis=1)
        mean = _group_sum_lanes(y, lo) * (1.0 / hd)
        yc = y - mean
        var = _group_sum_lanes(yc * yc, lo) * (1.0 / hd)
        y_s[rows, :] = (yc * lax.rsqrt(var + RWKV_GN_EPS) * lnw_ref[...] + lnb_ref[...]
                        + bonus_s[rows, :])
        return 0

    lax.fori_loop(0, tm // cs, chunk, 0, unroll=2)
    o_ref[0] = _gated_residual(y_s[...], g_s[...], x_ref[0], gate_ref[0], wout_ref[...])


def _rwkv_layer(x, gain, mod3, mu, w_in, w0, w1, w2, a0, a1, a2, k_k, k_a, r_k, ln_w, ln_b, w_out,
                tm=512):
    bsz, seq, d = x.shape
    tm = min(tm, seq)
    rank = w1.shape[1]
    pad_c = lambda w: jnp.pad(w, ((0, 0), (0, LANES - rank))).astype(BF16)
    pad_r = lambda w: jnp.pad(w, ((0, LANES - rank), (0, 0))).astype(BF16)
    vec = lambda p: p.reshape(1, d)
    tile = pl.BlockSpec((1, tm, d), lambda b, t: (b, t, 0))
    buf = pltpu.VMEM((tm, d), F32)
    return pl.pallas_call(
        _rwkv_kernel,
        out_shape=jax.ShapeDtypeStruct((bsz, seq, d), F32),
        grid=(bsz, seq // tm),
        in_specs=[tile, _resident((1, d)), *_mod_specs(d), _resident((SUBLANES, d)),
                  _resident((4, d, d)), _resident((d, LANES)), _resident((LANES, d)),
                  _resident((d, LANES)), _resident((LANES, d))]
                 + [_resident((1, d))] * 7
                 + [_resident((RWKV_CHUNK, RWKV_CHUNK)), _resident((d, d))],
        out_specs=tile,
        scratch_shapes=[buf] * 9 + [pltpu.VMEM((SUBLANES, d), F32),
                                    pltpu.VMEM((d // LANES, LANES, LANES), F32)],
        compiler_params=_params("parallel", "arbitrary"),
        name="rwkv_layer",
    )(x, gain.reshape(1, d), mod3, mod3, mod3,
      jnp.pad(mu, ((0, SUBLANES - mu.shape[0]), (0, 0))),
      w_in.astype(BF16), pad_c(w1), pad_r(w2), pad_c(a1), pad_r(a2), vec(w0), vec(a0),
      vec(k_k), vec(k_a), vec(r_k), vec(ln_w), vec(ln_b),
      _lower_tri_ones(RWKV_CHUNK), w_out.astype(BF16))


def _gla_kernel(x_ref, gain_ref, shift_ref, scale_ref, gate_ref, win_ref, w2_ref, ab_ref,
                ngain_ref, ltri_ref, wout_ref, o_ref, o_buf, state):
    tt = x_ref.shape[1]
    key_dim = w2_ref.shape[1]
    val_dim = wout_ref.shape[0]
    dk = key_dim // GLA_HEADS
    dv = val_dim // GLA_HEADS
    cs = GLA_CHUNK

    @pl.when(pl.program_id(1) == 0)
    def _():
        state[...] = jnp.zeros_like(state)

    h = _prenorm(x_ref[0], gain_ref[...], scale_ref[0], shift_ref[0]).astype(BF16)

    def col(off, n):
        return jnp.dot(h, win_ref[:, off:off + n], preferred_element_type=F32)

    q = col(0, key_dim)
    k = col(key_dim, key_dim)
    v = col(2 * key_dim, val_dim)
    g = col(2 * key_dim + val_dim, val_dim)
    a_low = col(2 * key_dim + 2 * val_dim, LANES)

    ri = lax.broadcasted_iota(jnp.int32, (cs, cs), 0)
    ci = lax.broadcasted_iota(jnp.int32, (cs, cs), 1)
    causal = ci <= ri
    ltri = ltri_ref[...]
    z = _dot(a_low, w2_ref[...]) + ab_ref[...]
    log_alpha = -_softplus(-z) * (1.0 / GLA_GATE_NORM)

    for c in range(tt // cs):
        rows = slice(c * cs, (c + 1) * cs)
        cum = _dot_exact_lhs(ltri, log_alpha[rows, :])
        last = cum[cs - 1:cs, :]
        kc = k[rows, :]
        q_dec = q[rows, :] * (dk ** -0.5) * jnp.exp(cum)
        k_inv = kc * jnp.exp(-cum)
        k_end = kc * jnp.exp(last - cum)
        decay = jnp.exp(last)
        for hh in range(GLA_HEADS):
            ks = slice(hh * dk, (hh + 1) * dk)
            vs = slice(hh * dv, (hh + 1) * dv)
            vh = v[rows, vs]
            st = state[hh]
            att = jnp.where(causal, _dot_nt(q_dec[:, ks], k_inv[:, ks]), 0.0)
            o = _dot(att, vh) + _dot_nt(q_dec[:, ks], st)
            state[hh] = st * decay[:, ks] + _dot(vh.T, k_end[:, ks])
            ms = jnp.mean(o * o, axis=-1, keepdims=True)
            o_buf[rows, vs] = o * lax.rsqrt(ms + RMS_EPS) * ngain_ref[...]

    o_ref[0] = _gated_residual(o_buf[...], g, x_ref[0], gate_ref[0], wout_ref[...])


def _gla_layer(x, gain, mod3, w_in, alpha_w2, alpha_b, norm_gain, w_out, tt=512):
    bsz, seq, d = x.shape
    tt = min(tt, seq)
    key_dim, val_dim = alpha_w2.shape[1], w_out.shape[0]
    dk, dv = key_dim // GLA_HEADS, val_dim // GLA_HEADS
    w = jnp.pad(w_in, ((0, 0), (0, LANES - GLA_GATE_RANK))).astype(BF16)
    w2 = jnp.pad(alpha_w2, ((0, LANES - GLA_GATE_RANK), (0, 0))).astype(BF16)
    tile = pl.BlockSpec((1, tt, d), lambda b, t: (b, t, 0))
    return pl.pallas_call(
        _gla_kernel,
        out_shape=jax.ShapeDtypeStruct((bsz, seq, d), F32),
        grid=(bsz, seq // tt),
        in_specs=[tile, _resident((1, d)), *_mod_specs(d), _resident(w.shape),
                  _resident((LANES, key_dim)), _resident((1, key_dim)), _resident((1, dv)),
                  _resident((GLA_CHUNK, GLA_CHUNK)), _resident((val_dim, d))],
        out_specs=tile,
        scratch_shapes=[pltpu.VMEM((tt, val_dim), F32), pltpu.VMEM((GLA_HEADS, dv, dk), F32)],
        compiler_params=_params("parallel", "arbitrary"),
        name="gla_layer",
    )(x, gain.reshape(1, d), mod3, mod3, mod3, w, w2, alpha_b.reshape(1, key_dim),
      norm_gain.reshape(1, dv), _lower_tri_ones(GLA_CHUNK), w_out.astype(BF16))


def kernel(x, c, ln_gain, mod_w, mod_b, dsa_w_in, dsa_q_gain, dsa_k_gain, dsa_w_out, lru_w_in, lru_conv_w, lru_conv_b, lru_gate_a_w, lru_gate_a_b, lru_gate_x_w, lru_gate_x_b, lru_lambda, lru_w_out, rwkv_mu, rwkv_w_in, rwkv_w0, rwkv_w1, rwkv_w2, rwkv_a0, rwkv_a1, rwkv_a2, rwkv_k_k, rwkv_k_a, rwkv_r_k, rwkv_ln_w, rwkv_ln_b, rwkv_w_out, gla_w_in, gla_alpha_w2, gla_alpha_b, gla_norm_gain, gla_w_out):
    depth = mod_w.shape[0]
    bsz, _, d = x.shape
    mod = _modulation(c, mod_w, mod_b)
    for layer in range(depth):
        mixer, r = layer % 4, layer // 4
        mod3 = mod[layer].reshape(bsz, 1, 3 * d)
        gain = ln_gain[layer]
        if mixer == 0:
            x = _dsa_layer(x, gain, mod3, dsa_w_in[r], dsa_q_gain[r], dsa_k_gain[r], dsa_w_out[r])
        elif mixer == 1:
            x = _lru_layer(x, gain, mod3, lru_w_in[r], lru_conv_w[r], lru_conv_b[r], lru_gate_a_w[r],
                           lru_gate_a_b[r], lru_gate_x_w[r], lru_gate_x_b[r], lru_lambda[r], lru_w_out[r])
        elif mixer == 2:
            x = _rwkv_layer(x, gain, mod3, rwkv_mu[r], rwkv_w_in[r], rwkv_w0[r], rwkv_w1[r], rwkv_w2[r],
                            rwkv_a0[r], rwkv_a1[r], rwkv_a2[r], rwkv_k_k[r], rwkv_k_a[r],
                            rwkv_r_k[r].reshape(-1), rwkv_ln_w[r], rwkv_ln_b[r], rwkv_w_out[r])
        else:
            x = _gla_layer(x, gain, mod3, gla_w_in[r], gla_alpha_w2[r], gla_alpha_b[r],
                           gla_norm_gain[r], gla_w_out[r])
    return x
```

```python
import functools

import jax
import jax.numpy as jnp
from jax import lax
from jax.experimental import pallas as pl
from jax.experimental.pallas import tpu as pltpu

F32 = jnp.float32
BF16 = jnp.bfloat16
HIGHEST = lax.Precision.HIGHEST

LANES = 128
SUBLANES = 8
VMEM_LIMIT_BYTES = 56 * 1024 * 1024

RMS_EPS = 1e-6
ROPE_THETA = 10000.0

DSA_HEADS = 16
DSA_KV_HEADS = 4
DSA_HEAD_DIM = 64
DSA_IDX_HEADS = 8
DSA_IDX_DIM = 128
DSA_TOPK = 256
DSA_QBLOCK = 128
DSA_KEY_CHUNK = 256
LOG2E = 1.4426950408889634
DSA_IDX_SCALE = (DSA_IDX_HEADS * DSA_IDX_DIM) ** -0.5

LRU_BLOCKS = 16
LRU_CONV = 4
LRU_C = 8.0
LRU_GROUP = 256

RWKV_HEAD_DIM = 64
RWKV_GN_EPS = 64e-5
RWKV_CHUNK = 64

GLA_HEADS = 4
GLA_GATE_RANK = 16
GLA_GATE_NORM = 16.0
GLA_CHUNK = 64

NEG_BIG = -1e30
NT_DIMS = (((1,), (1,)), ((), ()))


def _params(*semantics):
    return pltpu.CompilerParams(dimension_semantics=semantics,
                                vmem_limit_bytes=VMEM_LIMIT_BYTES)


def _resident(shape):
    return pl.BlockSpec(shape, lambda *_: (0,) * len(shape), pipeline_mode=pl.Buffered(1))


def _dot(a, b):
    return jnp.dot(a.astype(BF16), b.astype(BF16), preferred_element_type=F32)


def _dot_nt(a, b):
    return lax.dot_general(a.astype(BF16), b.astype(BF16), NT_DIMS,
                           preferred_element_type=F32)


def _split3(x):
    hi = x.astype(BF16)
    r1 = x - hi.astype(F32)
    mid = r1.astype(BF16)
    lo = (r1 - mid.astype(F32)).astype(BF16)
    return hi, mid, lo


def _dot_exact_lhs(m01, x):
    hi, mid, lo = _split3(x)
    return (jnp.dot(m01, hi, preferred_element_type=F32)
            + jnp.dot(m01, mid, preferred_element_type=F32)
            + jnp.dot(m01, lo, preferred_element_type=F32))


def _dot_exact_rhs(x, m01):
    hi, mid, lo = _split3(x)
    return (jnp.dot(hi, m01, preferred_element_type=F32)
            + jnp.dot(mid, m01, preferred_element_type=F32)
            + jnp.dot(lo, m01, preferred_element_type=F32))


def _group_sum(z, g01):
    cols = [_dot_exact_rhs(z[:, c * LANES:(c + 1) * LANES], g01)
            for c in range(z.shape[1] // LANES)]
    return cols[0] if len(cols) == 1 else jnp.concatenate(cols, axis=1)


def _group_sum_lanes(z, lo):
    cols = []
    for c in range(z.shape[1] // LANES):
        zb = z[:, c * LANES:(c + 1) * LANES]
        s_lo = jnp.sum(jnp.where(lo, zb, 0.0), axis=1, keepdims=True)
        s_hi = jnp.sum(jnp.where(lo, 0.0, zb), axis=1, keepdims=True)
        cols.append(jnp.where(lo, s_lo, s_hi))
    return cols[0] if len(cols) == 1 else jnp.concatenate(cols, axis=1)


def _silu(x):
    return x * jax.nn.sigmoid(x)


def _softplus(z):
    return jnp.maximum(z, 0.0) + jnp.log1p(jnp.exp(-jnp.abs(z)))


def _prenorm(x, gain, scale, shift):
    ms = jnp.mean(x * x, axis=-1, keepdims=True)
    y = x * lax.rsqrt(ms + RMS_EPS) * gain
    return y * (1.0 + scale) + shift


def _gated_residual(y, g, x, gate, w_out):
    a = (y * _silu(g)).astype(BF16)
    return x + gate * jnp.dot(a, w_out, preferred_element_type=F32)


def _mod_kernel(c_ref, w_ref, b_ref, o_ref):
    o_ref[0] = jnp.dot(_silu(c_ref[...]), w_ref[0], precision=HIGHEST,
                       preferred_element_type=F32) + b_ref[0]


def _modulation(c, mod_w, mod_b):
    depth, d, _ = mod_w.shape
    bsz = c.shape[0]
    return pl.pallas_call(
        _mod_kernel,
        out_shape=jax.ShapeDtypeStruct((depth, bsz, 3 * d), F32),
        grid=(depth, 3),
        in_specs=[pl.BlockSpec((bsz, d), lambda l, j: (0, 0)),
                  pl.BlockSpec((1, d, d), lambda l, j: (l, 0, j)),
                  pl.BlockSpec((1, 1, d), lambda l, j: (l, 0, j))],
        out_specs=pl.BlockSpec((1, bsz, d), lambda l, j: (l, 0, j)),
        compiler_params=_params("arbitrary", "arbitrary"),
        name="adaln_mod",
    )(c, mod_w, mod_b.reshape(depth, 1, 3 * d))


def _mod_specs(d):
    return [pl.BlockSpec((1, 1, d), lambda b, t, j=j: (b, 0, j)) for j in range(3)]


def _head_group_ones():
    r = jnp.arange(LANES) // DSA_HEAD_DIM
    return (r[:, None] == r[None, :]).astype(BF16)


def _lower_tri_ones(n):
    return (jnp.arange(n)[:, None] >= jnp.arange(n)[None, :]).astype(BF16)


def _rope_tables(seq, dim, reps):
    half = dim // 2
    inv_freq = ROPE_THETA ** (-jnp.arange(half, dtype=F32) / half)
    ang = jnp.arange(seq, dtype=F32)[:, None] * inv_freq[None, :]
    cos = jnp.concatenate([jnp.cos(ang), jnp.cos(ang)], axis=1)
    sin = jnp.concatenate([-jnp.sin(ang), jnp.sin(ang)], axis=1)
    return jnp.tile(cos, (1, reps)), jnp.tile(sin, (1, reps))


def _rope64(x, cos, sin, lane_lo):
    cols = []
    for c in range(x.shape[1] // LANES):
        xb = x[:, c * LANES:(c + 1) * LANES]
        rot = jnp.where(lane_lo, pltpu.roll(xb, 96, 1), pltpu.roll(xb, 32, 1))
        cols.append(xb * cos + rot * sin)
    return cols[0] if len(cols) == 1 else jnp.concatenate(cols, axis=1)


def _rope128(x, cos, sin):
    cols = []
    for c in range(x.shape[1] // LANES):
        xb = x[:, c * LANES:(c + 1) * LANES]
        cols.append(xb * cos + pltpu.roll(xb, 64, 1) * sin)
    return cols[0] if len(cols) == 1 else jnp.concatenate(cols, axis=1)


_DSA_Q = DSA_HEADS * DSA_HEAD_DIM
_DSA_KV2 = DSA_KV_HEADS * LANES
_DSA_QI = DSA_IDX_HEADS * DSA_IDX_DIM
_DSA_OFF_Q = 0
_DSA_OFF_G = _DSA_OFF_Q + _DSA_Q
_DSA_OFF_QI = _DSA_OFF_G + _DSA_Q
_DSA_OFF_K = _DSA_OFF_QI + _DSA_QI
_DSA_OFF_V = _DSA_OFF_K + _DSA_KV2
_DSA_OFF_KI = _DSA_OFF_V + _DSA_KV2
_DSA_OFF_WI = _DSA_OFF_KI + DSA_IDX_DIM
_DSA_COLS = _DSA_OFF_WI + LANES


def _dsa_proj_kernel(x_ref, gain_ref, shift_ref, scale_ref, w_ref, qgain_ref, kgain_ref,
                     cos64_ref, sin64_ref, cos128_ref, sin128_ref, g01_ref,
                     q_ref, g_ref, qi_ref, k_ref, v_ref, ki_ref, wi_ref):
    h = _prenorm(x_ref[0], gain_ref[...], scale_ref[0], shift_ref[0]).astype(BF16)
    g01 = g01_ref[...]
    cos64, sin64 = cos64_ref[...], sin64_ref[...]
    cos128, sin128 = cos128_ref[...], sin128_ref[...]
    lane = lax.broadcasted_iota(jnp.int32, cos64.shape, 1)
    lane_lo = (lane % DSA_HEAD_DIM) < DSA_HEAD_DIM // 2

    def head_norm_rope(raw, gain):
        ms = _group_sum(raw * raw, g01) * (1.0 / DSA_HEAD_DIM)
        return _rope64(raw * lax.rsqrt(ms + RMS_EPS) * gain, cos64, sin64, lane_lo)

    def col(off, n):
        return jnp.dot(h, w_ref[:, off:off + n], preferred_element_type=F32)

    tm = h.shape[0]
    row_lo64 = lax.broadcasted_iota(jnp.int32, (LANES, tm), 0) < DSA_HEAD_DIM
    q = head_norm_rope(col(_DSA_OFF_Q, _DSA_Q), qgain_ref[...]) * (LOG2E * DSA_HEAD_DIM ** -0.5)
    qt = q.T
    for hh in range(DSA_HEADS):
        blk = qt[(hh // 2) * LANES:(hh // 2 + 1) * LANES, :]
        keep = row_lo64 if hh % 2 == 0 else ~row_lo64
        q_ref[0, hh] = jnp.where(keep, blk, 0.0).astype(BF16)
    g_ref[0] = col(_DSA_OFF_G, _DSA_Q).astype(BF16)
    qit = _rope128(col(_DSA_OFF_QI, _DSA_QI), cos128, sin128).T.astype(BF16)
    for hh in range(DSA_IDX_HEADS):
        qi_ref[0, hh] = qit[hh * DSA_IDX_DIM:(hh + 1) * DSA_IDX_DIM, :]
    k2 = head_norm_rope(col(_DSA_OFF_K, _DSA_KV2), kgain_ref[...]).astype(BF16)
    for kv in range(DSA_KV_HEADS):
        k_ref[0, kv] = k2[:, kv * LANES:(kv + 1) * LANES]
    vt = col(_DSA_OFF_V, _DSA_KV2).T
    for kv in range(DSA_KV_HEADS):
        v_ref[0, kv] = jnp.where(row_lo64, vt[kv * LANES:(kv + 1) * LANES, :], 1.0).astype(BF16)
    ki_ref[0] = _rope128(col(_DSA_OFF_KI, DSA_IDX_DIM), cos128, sin128).astype(BF16)
    wit = (col(_DSA_OFF_WI, LANES) * DSA_IDX_SCALE).T
    wi_ref[0] = wit[0:DSA_IDX_HEADS, :]


def _dsa_weights(w_in):
    d = w_in.shape[0]
    q_end = _DSA_Q
    k_end = q_end + DSA_KV_HEADS * DSA_HEAD_DIM
    v_end = k_end + DSA_KV_HEADS * DSA_HEAD_DIM
    g_end = v_end + _DSA_Q
    qi_end = g_end + _DSA_QI
    wi_end = qi_end + DSA_IDX_HEADS
    wq, wk, wv, wg = w_in[:, :q_end], w_in[:, q_end:k_end], w_in[:, k_end:v_end], w_in[:, v_end:g_end]
    wqi, wwi, wki = w_in[:, g_end:qi_end], w_in[:, qi_end:wi_end], w_in[:, wi_end:]

    def dup(w):
        w = w.reshape(d, DSA_KV_HEADS, 1, DSA_HEAD_DIM)
        return jnp.broadcast_to(w, (d, DSA_KV_HEADS, 2, DSA_HEAD_DIM)).reshape(d, _DSA_KV2)

    wwi = jnp.pad(wwi, ((0, 0), (0, LANES - DSA_IDX_HEADS)))
    return jnp.concatenate([wq, wg, wqi, dup(wk), dup(wv), wki, wwi], axis=1).astype(BF16)


def _dsa_project(x, gain, mod3, w_in, q_gain, k_gain, tm=512):
    bsz, seq, d = x.shape
    tm = min(tm, seq)
    w = _dsa_weights(w_in)
    cos64, sin64 = _rope_tables(seq, DSA_HEAD_DIM, LANES // DSA_HEAD_DIM)
    cos128, sin128 = _rope_tables(seq, DSA_IDX_DIM, 1)
    shift_spec, scale_spec, _ = _mod_specs(d)
    table = pl.BlockSpec((tm, LANES), lambda b, t: (t, 0))
    row = lambda n, dt: jax.ShapeDtypeStruct((bsz, seq, n), dt)
    heads = lambda n: jax.ShapeDtypeStruct((bsz, n, seq, LANES), BF16)
    head_spec = lambda n: pl.BlockSpec((1, n, tm, LANES), lambda b, t: (b, 0, t, 0))
    heads_t = lambda n: jax.ShapeDtypeStruct((bsz, n, LANES, seq), BF16)
    head_t_spec = lambda n: pl.BlockSpec((1, n, LANES, tm), lambda b, t: (b, 0, 0, t))
    return pl.pallas_call(
        _dsa_proj_kernel,
        out_shape=[heads_t(DSA_HEADS), row(_DSA_Q, BF16), heads_t(DSA_IDX_HEADS),
                   heads(DSA_KV_HEADS), heads_t(DSA_KV_HEADS), row(DSA_IDX_DIM, BF16),
                   jax.ShapeDtypeStruct((bsz, DSA_IDX_HEADS, seq), F32)],
        grid=(bsz, seq // tm),
        in_specs=[pl.BlockSpec((1, tm, d), lambda b, t: (b, t, 0)),
                  _resident((1, d)), shift_spec, scale_spec, _resident(w.shape),
                  _resident((1, _DSA_Q)), _resident((1, _DSA_KV2)),
                  table, table, table, table, _resident((LANES, LANES))],
        out_specs=[head_t_spec(DSA_HEADS),
                   pl.BlockSpec((1, tm, _DSA_Q), lambda b, t: (b, t, 0)),
                   head_t_spec(DSA_IDX_HEADS),
                   head_spec(DSA_KV_HEADS),
                   head_t_spec(DSA_KV_HEADS),
                   pl.BlockSpec((1, tm, DSA_IDX_DIM), lambda b, t: (b, t, 0)),
                   pl.BlockSpec((1, DSA_IDX_HEADS, tm), lambda b, t: (b, 0, t))],
        compiler_params=_params("parallel", "parallel"),
        name="dsa_proj",
    )(x, gain.reshape(1, d), mod3, mod3, w,
      jnp.tile(q_gain, DSA_HEADS).reshape(1, _DSA_Q),
      jnp.tile(k_gain, _DSA_KV2 // DSA_HEAD_DIM).reshape(1, _DSA_KV2),
      cos64, sin64, cos128, sin128, _head_group_ones())


def _reduce_rows(x, op):
    part = op(x.reshape(x.shape[0] // 64, 64, x.shape[1]), axis=0)
    return op(part, axis=0, keepdims=True)


def _sortable_to_float(key):
    bits = jnp.where(key >= 0, key, key ^ jnp.int32(0x7FFFFFFF))
    return lax.bitcast_convert_type(bits, F32)


def _dsa_attn_kernel(n_sel, chunk, qi_ref, wi_ref, ki_ref, q_ref, k_ref, v_ref, tri_ref,
                     g_ref, x_ref, gate_ref, wout_ref, o_ref, score_ref, bias_ref, qk_ref):
    seq, qb = score_ref.shape
    per = chunk // qb
    needed = lax.div(pl.program_id(1) + per, per)
    for j in range(seq // chunk):
        pl.when(needed == j + 1)(functools.partial(
            _dsa_attn_block, n_sel, chunk * (j + 1), qi_ref, wi_ref, ki_ref, q_ref, k_ref, v_ref,
            tri_ref, g_ref, x_ref, gate_ref, wout_ref, o_ref, score_ref, bias_ref, qk_ref))


def _dsa_attn_block(n_sel, width, qi_ref, wi_ref, ki_ref, q_ref, k_ref, v_ref, tri_ref,
                    g_ref, x_ref, gate_ref, wout_ref, o_ref, score_ref, bias_ref, qk_ref):
    blk = pl.program_id(1)
    qb = score_ref.shape[1]
    ki = ki_ref[0, 0:width, :]
    wi = wi_ref[0]
    stacked = DSA_HEADS // DSA_KV_HEADS

    def heads_of(ref, first):
        return jnp.concatenate([ref[0, first + j] for j in range(stacked)], axis=1)

    acc = None
    for grp in range(DSA_IDX_HEADS // stacked):
        logits = jnp.dot(ki, heads_of(qi_ref, grp * stacked), preferred_element_type=F32)
        for j in range(stacked):
            hh = grp * stacked + j
            term = wi[hh:hh + 1, :] * jnp.maximum(logits[:, j * qb:(j + 1) * qb], 0.0)
            acc = term if acc is None else acc + term
    q_pos = blk * qb + lax.broadcasted_iota(jnp.int32, (1, qb), 1)
    key_pos = lax.broadcasted_iota(jnp.int32, (width, 1), 0)
    score_ref[0:width, :] = jnp.where(key_pos <= q_pos, acc, -jnp.inf)

    k_sel = jnp.float32(n_sel)

    def count_ge(thr):
        return _reduce_rows(jnp.where(score_ref[0:width, :] >= thr, 1.0, 0.0), jnp.sum)

    def reaches(cand):
        return jnp.where(count_ge(_sortable_to_float(cand)) >= k_sel, 1, 0)

    def q_heads(kv):
        return jnp.concatenate([q_ref[0, kv * stacked + j] for j in range(stacked)], axis=1)

    half = width // 2
    per_step = 4

    def search_step(i, key):
        kv = i // 2
        rows = pl.ds(pl.multiple_of((i % 2) * half, half), half)
        qk_ref[kv, rows, :] = jnp.dot(k_ref[0, kv, rows, :], q_heads(kv),
                                      preferred_element_type=F32)
        for j in range(per_step):
            n = i * per_step + j
            cand = jnp.where(n == 0, jnp.int32(0),
                             key + lax.shift_left(jnp.int32(1), jnp.int32(31) - n))
            key = jnp.where(reaches(cand) > 0, cand, key)
        return key

    key0 = jnp.full((1, qb), -2 ** 31, jnp.int32)
    thr = _sortable_to_float(lax.fori_loop(0, 2 * DSA_KV_HEADS, search_step, key0))

    score = score_ref[0:width, :]
    gt = score > thr
    need = k_sel - _reduce_rows(jnp.where(gt, 1.0, 0.0), jnp.sum)
    take_all = q_pos < n_sel
    tri = tri_ref[...]
    run = jnp.zeros((1, qb), F32)
    for c in range(width // LANES):
        sl = slice(c * LANES, (c + 1) * LANES)
        eq = jnp.where(score[sl, :] == thr, 1.0, 0.0)
        incl = jnp.dot(tri, eq.astype(BF16), preferred_element_type=F32)
        tie_ok = (incl - eq + run) < need
        run = run + incl[LANES - 1:LANES, :]
        sel = gt[sl, :] | ((eq > 0.0) & tie_ok) | take_all
        causal = key_pos[sl, :] <= q_pos
        bias_ref[sl, :] = jnp.where(sel & causal, 0.0, NEG_BIG)

    bias = bias_ref[0:width, :]
    outs = []
    for kv in range(DSA_KV_HEADS):
        s4 = qk_ref[kv, 0:width, :]
        ps = []
        for j in range(stacked):
            s = s4[:, j * qb:(j + 1) * qb] + bias
            ps.append(jnp.exp2(s - _reduce_rows(s, jnp.max)).astype(BF16))
        ov = jnp.dot(v_ref[0, kv, :, 0:width], jnp.concatenate(ps, axis=1),
                     preferred_element_type=F32)
        ov = ov[0:DSA_HEAD_DIM, :] / ov[DSA_HEAD_DIM:DSA_HEAD_DIM + 1, :]
        for c in range(stacked // 2):
            even = ov[:, (2 * c) * qb:(2 * c + 1) * qb]
            odd = ov[:, (2 * c + 1) * qb:(2 * c + 2) * qb]
            outs.append(jnp.concatenate([even, odd], axis=0).T)
    o_ref[0] = _gated_residual(jnp.concatenate(outs, axis=1), g_ref[0].astype(F32), x_ref[0],
                               gate_ref[0], wout_ref[...])


def _dsa_attention(q, qi, k2, vt, ki, wi, g, x, mod3, w_out):
    bsz, seq, d = x.shape
    qb = DSA_QBLOCK
    n_sel = min(DSA_TOPK, seq // 4)
    chunk = min(DSA_KEY_CHUNK, seq // 2)
    return pl.pallas_call(
        functools.partial(_dsa_attn_kernel, n_sel, chunk),
        out_shape=jax.ShapeDtypeStruct((bsz, seq, d), F32),
        grid=(bsz, seq // qb),
        in_specs=[pl.BlockSpec((1, DSA_IDX_HEADS, LANES, qb), lambda b, i: (b, 0, 0, i)),
                  pl.BlockSpec((1, DSA_IDX_HEADS, qb), lambda b, i: (b, 0, i)),
                  pl.BlockSpec((1, seq, DSA_IDX_DIM), lambda b, i: (b, 0, 0)),
                  pl.BlockSpec((1, DSA_HEADS, LANES, qb), lambda b, i: (b, 0, 0, i)),
                  pl.BlockSpec((1, DSA_KV_HEADS, seq, LANES), lambda b, i: (b, 0, 0, 0)),
                  pl.BlockSpec((1, DSA_KV_HEADS, LANES, seq), lambda b, i: (b, 0, 0, 0)),
                  _resident((LANES, LANES)),
                  pl.BlockSpec((1, qb, _DSA_Q), lambda b, i: (b, i, 0)),
                  pl.BlockSpec((1, qb, d), lambda b, i: (b, i, 0)),
                  _mod_specs(d)[2],
                  _resident((_DSA_Q, d))],
        out_specs=pl.BlockSpec((1, qb, d), lambda b, i: (b, i, 0)),
        scratch_shapes=[pltpu.VMEM((seq, qb), F32), pltpu.VMEM((seq, qb), F32),
                        pltpu.VMEM((DSA_KV_HEADS, seq, (DSA_HEADS // DSA_KV_HEADS) * qb), F32)],
        compiler_params=_params("parallel", "parallel"),
        name="dsa_attn",
    )(qi, wi, ki, q, k2, vt, _lower_tri_ones(LANES), g, x, mod3, w_out.astype(BF16))


def _dsa_layer(x, gain, mod3, w_in, q_gain, k_gain, w_out):
    q, g, qi, k2, vt, ki, wi = _dsa_project(x, gain, mod3, w_in, q_gain, k_gain)
    return _dsa_attention(q, qi, k2, vt, ki, wi, g, x, mod3, w_out)


def _lru_kernel(x_ref, gain_ref, shift_ref, scale_ref, gate_ref, win_ref, cw_ref, cb_ref,
                wa_ref, ba_ref, wx_ref, bx_ref, lam_ref, wout_ref, o_ref,
                h_buf, ubuf, a_buf, b_buf, hs_buf, g_buf, h_carry):
    tt = x_ref.shape[1]
    width = ubuf.shape[1]
    halo = SUBLANES

    @pl.when(pl.program_id(1) == 0)
    def _():
        ubuf[0:halo, :] = jnp.zeros((halo, width), F32)
        h_carry[...] = jnp.zeros_like(h_carry)

    h_buf[...] = _prenorm(x_ref[0], gain_ref[...], scale_ref[0], shift_ref[0]).astype(BF16)
    ubuf[halo:halo + tt, :] = jnp.dot(h_buf[...], win_ref[:, 0:width], preferred_element_type=F32)
    cw = cw_ref[...]
    u = cb_ref[...]
    for j in range(LRU_CONV):
        start = halo - (LRU_CONV - 1) + j
        u = u + cw[j:j + 1, :] * ubuf[start:start + tt, :]
    ubuf[0:halo, :] = ubuf[tt:tt + halo, :]

    g_buf[...] = jnp.dot(h_buf[...], win_ref[:, width:2 * width], preferred_element_type=F32)

    sp = _softplus(-lam_ref[...])
    for c in range(width // LRU_GROUP):
        sl = slice(c * LRU_GROUP, (c + 1) * LRU_GROUP)
        uc = u[:, sl]
        ub = uc.astype(BF16)
        r = jax.nn.sigmoid(jnp.dot(ub, wa_ref[c], preferred_element_type=F32) + ba_ref[:, sl])
        i = jax.nn.sigmoid(jnp.dot(ub, wx_ref[c], preferred_element_type=F32) + bx_ref[:, sl])
        a = jnp.exp(-LRU_C * r * sp[:, sl])
        a_buf[:, sl] = a
        b_buf[:, sl] = jnp.sqrt(1.0 - a * a) * (i * uc)

    row = lax.broadcasted_iota(jnp.int32, (SUBLANES, width), 0)

    def group(gi, h_prev):
        r0 = pl.multiple_of(gi * SUBLANES, SUBLANES)
        a = a_buf[pl.ds(r0, SUBLANES), :]
        b = b_buf[pl.ds(r0, SUBLANES), :]
        for s in (1, 2, 4):
            ok = row >= s
            b = jnp.where(ok, a * pltpu.roll(b, s, 0) + b, b)
            a = jnp.where(ok, a * pltpu.roll(a, s, 0), a)
        h = a * h_prev + b
        hs_buf[pl.ds(r0, SUBLANES), :] = h
        return jnp.broadcast_to(h[SUBLANES - 1:SUBLANES, :], (SUBLANES, width))

    h_carry[...] = lax.fori_loop(0, tt // SUBLANES, group, h_carry[...])
    o_ref[0] = _gated_residual(hs_buf[...], g_buf[...], x_ref[0], gate_ref[0], wout_ref[...])


def _lru_gate_blocks(w):
    nb, bd, _ = w.shape
    per = LRU_GROUP // bd
    w = w.reshape(nb // per, per, bd, bd)
    eye = jnp.eye(per, dtype=w.dtype)
    return jnp.einsum('gpcd,pq->gpcqd', w, eye).reshape(nb // per, LRU_GROUP, LRU_GROUP).astype(BF16)


def _lru_layer(x, gain, mod3, w_in, conv_w, conv_b, gate_a_w, gate_a_b, gate_x_w, gate_x_b, lam,
               w_out, tt=512):
    bsz, seq, d = x.shape
    tt = min(tt, seq)
    width = w_in.shape[1] // 2
    ngroups = width // LRU_GROUP
    tile = pl.BlockSpec((1, tt, d), lambda b, t: (b, t, 0))
    vec = lambda: _resident((1, width))
    gatew = lambda: _resident((ngroups, LRU_GROUP, LRU_GROUP))
    cw = jnp.pad(conv_w, ((0, SUBLANES - LRU_CONV), (0, 0)))
    return pl.pallas_call(
        _lru_kernel,
        out_shape=jax.ShapeDtypeStruct((bsz, seq, d), F32),
        grid=(bsz, seq // tt),
        in_specs=[tile, _resident((1, d)), *_mod_specs(d), _resident((d, 2 * width)),
                  _resident((SUBLANES, width)), vec(), gatew(), vec(), gatew(), vec(), vec(),
                  _resident((width, d))],
        out_specs=tile,
        scratch_shapes=[pltpu.VMEM((tt, d), BF16),
                        pltpu.VMEM((tt + SUBLANES, width), F32),
                        pltpu.VMEM((tt, width), F32), pltpu.VMEM((tt, width), F32),
                        pltpu.VMEM((tt, width), F32), pltpu.VMEM((tt, width), F32),
                        pltpu.VMEM((SUBLANES, width), F32)],
        compiler_params=_params("parallel", "arbitrary"),
        name="rglru_layer",
    )(x, gain.reshape(1, d), mod3, mod3, mod3, w_in.astype(BF16), cw, conv_b.reshape(1, width),
      _lru_gate_blocks(gate_a_w), gate_a_b.reshape(1, width),
      _lru_gate_blocks(gate_x_w), gate_x_b.reshape(1, width), lam.reshape(1, width),
      w_out.astype(BF16))


def _rwkv_kernel(x_ref, gain_ref, shift_ref, scale_ref, gate_ref, mu_ref, w_ref, w1_ref, w2_ref,
                 a1_ref, a2_ref, w0_ref, a0_ref, kkw_ref, ka_ref, rk_ref, lnw_ref, lnb_ref,
                 ltri_ref, wout_ref, o_ref,
                 r_s, lw_s, k_s, v_s, kk_s, a_s, bonus_s, g_s, y_s, carry, state):
    tm = x_ref.shape[1]
    d = x_ref.shape[2]
    cs = RWKV_CHUNK
    hd = RWKV_HEAD_DIM

    @pl.when(pl.program_id(1) == 0)
    def _():
        carry[...] = jnp.zeros_like(carry)
        state[...] = jnp.zeros_like(state)

    lo_t = lax.broadcasted_iota(jnp.int32, (tm, LANES), 1) < hd
    h = _prenorm(x_ref[0], gain_ref[...], scale_ref[0], shift_ref[0])
    first = lax.broadcasted_iota(jnp.int32, (tm, d), 0) == 0
    h_prev = jnp.where(first, carry[0:1, :], pltpu.roll(h, 1, 0))
    carry[...] = jnp.broadcast_to(h[tm - 1:tm, :], carry.shape)
    delta = h_prev - h
    mu = mu_ref[...]

    def mix(n):
        return (h + delta * mu[n:n + 1, :]).astype(BF16)

    r = jnp.dot(mix(0), w_ref[0], preferred_element_type=F32)
    k = jnp.dot(mix(1), w_ref[1], preferred_element_type=F32)
    v = jnp.dot(mix(2), w_ref[2], preferred_element_type=F32)
    g_s[...] = jnp.dot(mix(3), w_ref[3], preferred_element_type=F32)
    w_lora = _dot(jnp.tanh(jnp.dot(mix(4), w1_ref[...], preferred_element_type=F32)), w2_ref[...])
    w_log = -_softplus(-(w0_ref[...] + w_lora)) - 0.5
    lw_s[...] = -jnp.exp(w_log)
    a_lora = _dot(jnp.dot(mix(5), a1_ref[...], preferred_element_type=F32), a2_ref[...])
    a = jax.nn.sigmoid(a0_ref[...] + a_lora)
    kk = k * kkw_ref[...]
    kk = kk / jnp.maximum(jnp.sqrt(_group_sum_lanes(kk * kk, lo_t)), 1e-12)
    k = k * (1.0 + (a - 1.0) * ka_ref[...])
    r_s[...] = r
    k_s[...] = k
    v_s[...] = v
    kk_s[...] = kk
    a_s[...] = a
    bonus_s[...] = _group_sum_lanes(r * k * rk_ref[...], lo_t) * v

    lane = lax.broadcasted_iota(jnp.int32, (cs, LANES), 1)
    lo = lane < hd
    ri = lax.broadcasted_iota(jnp.int32, (LANES, LANES), 0)
    ci = lax.broadcasted_iota(jnp.int32, (LANES, LANES), 1)
    same = (ri < cs) == (ci < cs)
    strict = same & (ci < ri)
    incl = same & (ci <= ri)
    eye = jnp.where(ri == ci, 1.0, 0.0)
    ltri = ltri_ref[...]

    def stack(z):
        return jnp.concatenate([z, z], axis=0)

    def split_heads(z):
        return jnp.concatenate([jnp.where(lo, z, 0.0), jnp.where(lo, 0.0, z)], axis=0)

    def own(z):
        return jnp.where(lo, z[0:cs], z[cs:2 * cs])

    pairs = range(d // LANES)
    lanes = [slice(p * LANES, (p + 1) * LANES) for p in pairs]

    def chunk(c, _):
        rows = pl.ds(pl.multiple_of(c * cs, cs), cs)
        lw = lw_s[rows, :]
        g_inc = _dot_exact_lhs(ltri, lw)
        g_last = g_inc[cs - 1:cs, :]
        kk = kk_s[rows, :]
        kc = k_s[rows, :]
        vc = v_s[rows, :]
        bvec = kk * a_s[rows, :]
        e_neg = jnp.exp(-g_inc)
        e_end = jnp.exp(g_last - g_inc)
        at = -kk * jnp.exp(g_inc - lw)
        rt = r_s[rows, :] * jnp.exp(g_inc)
        bt = bvec * e_neg
        kt = kc * e_neg
        bend = (bvec * e_end).astype(BF16)
        kend = (kc * e_end).astype(BF16)
        decay = jnp.exp(g_last)

        lhs = [jnp.concatenate([at[:, s], rt[:, s]], axis=0).astype(BF16) for s in lanes]
        nb = [_dot_nt(lhs[p], split_heads(bt[:, lanes[p]])) for p in pairs]
        nk = [_dot_nt(lhs[p], split_heads(kt[:, lanes[p]])) for p in pairs]
        n_ab = [jnp.where(strict, stack(nb[p][0:cs]), 0.0) for p in pairs]
        n_rb = [jnp.where(incl, stack(nb[p][cs:2 * cs]), 0.0).astype(BF16) for p in pairs]
        n_ak = [jnp.where(strict, stack(nk[p][0:cs]), 0.0) for p in pairs]
        n_rk = [jnp.where(incl, stack(nk[p][cs:2 * cs]), 0.0) for p in pairs]

        inv = [eye + n_ab[p] for p in pairs]
        pw = [_dot(n_ab[p], n_ab[p]) for p in pairs]
        for _stage in range(4):
            res = [_dot(pw[p], jnp.concatenate([pw[p], inv[p]], axis=1)) for p in pairs]
            pw = [res[p][:, 0:LANES] for p in pairs]
            inv = [inv[p] + res[p][:, LANES:2 * LANES] for p in pairs]
        inv = [(inv[p] + _dot(pw[p], inv[p])).astype(BF16) for p in pairs]

        vv = [stack(vc[:, s]).astype(BF16) for s in lanes]
        w1 = [own(_dot(n_ak[p], vv[p])) for p in pairs]
        tz = [_dot(inv[p], jnp.concatenate([stack(at[:, lanes[p]]), stack(w1[p])], axis=1))
              for p in pairs]
        a2 = [own(tz[p][:, 0:LANES]) for p in pairs]
        u0 = [own(tz[p][:, LANES:2 * LANES]) for p in pairs]
        rz = [_dot(n_rb[p], jnp.concatenate([stack(a2[p]), stack(u0[p])], axis=1)) for p in pairs]
        r2 = [rt[:, lanes[p]] + own(rz[p][:, 0:LANES]) for p in pairs]
        y0 = [own(rz[p][:, LANES:2 * LANES]) + own(_dot(n_rk[p], vv[p])) for p in pairs]
        mlr = [jnp.where(same, _dot(a2[p].T, bend[:, lanes[p]]), 0.0) for p in pairs]
        c0 = [jnp.where(same, _dot(jnp.concatenate([u0[p], vc[:, lanes[p]]], axis=0).T,
                                   jnp.concatenate([bend[:, lanes[p]], kend[:, lanes[p]]], axis=0)), 0.0)
              for p in pairs]

        ys = []
        for p in pairs:
            s_bd = state[p]
            ys.append(_dot_nt(r2[p], s_bd) + y0[p])
            state[p] = s_bd * decay[:, lanes[p]] + _dot(s_bd, mlr[p]) + c0[p]
        y = jnp.concatenate(ys, axis=1)
        mean = _group_sum_lanes(y, lo) * (1.0 / hd)
        yc = y - mean
        var = _group_sum_lanes(yc * yc, lo) * (1.0 / hd)
        y_s[rows, :] = (yc * lax.rsqrt(var + RWKV_GN_EPS) * lnw_ref[...] + lnb_ref[...]
                        + bonus_s[rows, :])
        return 0

    lax.fori_loop(0, tm // cs, chunk, 0, unroll=2)
    o_ref[0] = _gated_residual(y_s[...], g_s[...], x_ref[0], gate_ref[0], wout_ref[...])


def _rwkv_layer(x, gain, mod3, mu, w_in, w0, w1, w2, a0, a1, a2, k_k, k_a, r_k, ln_w, ln_b, w_out,
                tm=512):
    bsz, seq, d = x.shape
    tm = min(tm, seq)
    rank = w1.shape[1]
    pad_c = lambda w: jnp.pad(w, ((0, 0), (0, LANES - rank))).astype(BF16)
    pad_r = lambda w: jnp.pad(w, ((0, LANES - rank), (0, 0))).astype(BF16)
    vec = lambda p: p.reshape(1, d)
    tile = pl.BlockSpec((1, tm, d), lambda b, t: (b, t, 0))
    buf = pltpu.VMEM((tm, d), F32)
    return pl.pallas_call(
        _rwkv_kernel,
        out_shape=jax.ShapeDtypeStruct((bsz, seq, d), F32),
        grid=(bsz, seq // tm),
        in_specs=[tile, _resident((1, d)), *_mod_specs(d), _resident((SUBLANES, d)),
                  _resident((4, d, d)), _resident((d, LANES)), _resident((LANES, d)),
                  _resident((d, LANES)), _resident((LANES, d))]
                 + [_resident((1, d))] * 7
                 + [_resident((RWKV_CHUNK, RWKV_CHUNK)), _resident((d, d))],
        out_specs=tile,
        scratch_shapes=[buf] * 9 + [pltpu.VMEM((SUBLANES, d), F32),
                                    pltpu.VMEM((d // LANES, LANES, LANES), F32)],
        compiler_params=_params("parallel", "arbitrary"),
        name="rwkv_layer",
    )(x, gain.reshape(1, d), mod3, mod3, mod3,
      jnp.pad(mu, ((0, SUBLANES - mu.shape[0]), (0, 0))),
      w_in.astype(BF16), pad_c(w1), pad_r(w2), pad_c(a1), pad_r(a2), vec(w0), vec(a0),
      vec(k_k), vec(k_a), vec(r_k), vec(ln_w), vec(ln_b),
      _lower_tri_ones(RWKV_CHUNK), w_out.astype(BF16))


def _gla_kernel(x_ref, gain_ref, shift_ref, scale_ref, gate_ref, win_ref, w2_ref, ab_ref,
                ngain_ref, ltri_ref, wout_ref, o_ref, o_buf, state):
    tt = x_ref.shape[1]
    key_dim = w2_ref.shape[1]
    val_dim = wout_ref.shape[0]
    dk = key_dim // GLA_HEADS
    dv = val_dim // GLA_HEADS
    cs = GLA_CHUNK

    @pl.when(pl.program_id(1) == 0)
    def _():
        state[...] = jnp.zeros_like(state)

    h = _prenorm(x_ref[0], gain_ref[...], scale_ref[0], shift_ref[0]).astype(BF16)

    def col(off, n):
        return jnp.dot(h, win_ref[:, off:off + n], preferred_element_type=F32)

    q = col(0, key_dim)
    k = col(key_dim, key_dim)
    v = col(2 * key_dim, val_dim)
    g = col(2 * key_dim + val_dim, val_dim)
    a_low = col(2 * key_dim + 2 * val_dim, LANES)

    ri = lax.broadcasted_iota(jnp.int32, (cs, cs), 0)
    ci = lax.broadcasted_iota(jnp.int32, (cs, cs), 1)
    causal = ci <= ri
    ltri = ltri_ref[...]
    z = _dot(a_low, w2_ref[...]) + ab_ref[...]
    log_alpha = -_softplus(-z) * (1.0 / GLA_GATE_NORM)

    for c in range(tt // cs):
        rows = slice(c * cs, (c + 1) * cs)
        cum = _dot_exact_lhs(ltri, log_alpha[rows, :])
        last = cum[cs - 1:cs, :]
        kc = k[rows, :]
        q_dec = q[rows, :] * (dk ** -0.5) * jnp.exp(cum)
        k_inv = kc * jnp.exp(-cum)
        k_end = kc * jnp.exp(last - cum)
        decay = jnp.exp(last)
        for hh in range(GLA_HEADS):
            ks = slice(hh * dk, (hh + 1) * dk)
            vs = slice(hh * dv, (hh + 1) * dv)
            vh = v[rows, vs]
            st = state[hh]
            att = jnp.where(causal, _dot_nt(q_dec[:, ks], k_inv[:, ks]), 0.0)
            o = _dot(att, vh) + _dot_nt(q_dec[:, ks], st)
            state[hh] = st * decay[:, ks] + _dot(vh.T, k_end[:, ks])
            ms = jnp.mean(o * o, axis=-1, keepdims=True)
            o_buf[rows, vs] = o * lax.rsqrt(ms + RMS_EPS) * ngain_ref[...]

    o_ref[0] = _gated_residual(o_buf[...], g, x_ref[0], gate_ref[0], wout_ref[...])


def _gla_layer(x, gain, mod3, w_in, alpha_w2, alpha_b, norm_gain, w_out, tt=512):
    bsz, seq, d = x.shape
    tt = min(tt, seq)
    key_dim, val_dim = alpha_w2.shape[1], w_out.shape[0]
    dk, dv = key_dim // GLA_HEADS, val_dim // GLA_HEADS
    w = jnp.pad(w_in, ((0, 0), (0, LANES - GLA_GATE_RANK))).astype(BF16)
    w2 = jnp.pad(alpha_w2, ((0, LANES - GLA_GATE_RANK), (0, 0))).astype(BF16)
    tile = pl.BlockSpec((1, tt, d), lambda b, t: (b, t, 0))
    return pl.pallas_call(
        _gla_kernel,
        out_shape=jax.ShapeDtypeStruct((bsz, seq, d), F32),
        grid=(bsz, seq // tt),
        in_specs=[tile, _resident((1, d)), *_mod_specs(d), _resident(w.shape),
                  _resident((LANES, key_dim)), _resident((1, key_dim)), _resident((1, dv)),
                  _resident((GLA_CHUNK, GLA_CHUNK)), _resident((val_dim, d))],
        out_specs=tile,
        scratch_shapes=[pltpu.VMEM((tt, val_dim), F32), pltpu.VMEM((GLA_HEADS, dv, dk), F32)],
        compiler_params=_params("parallel", "arbitrary"),
        name="gla_layer",
    )(x, gain.reshape(1, d), mod3, mod3, mod3, w, w2, alpha_b.reshape(1, key_dim),
      norm_gain.reshape(1, dv), _lower_tri_ones(GLA_CHUNK), w_out.astype(BF16))


def kernel(x, c, ln_gain, mod_w, mod_b, dsa_w_in, dsa_q_gain, dsa_k_gain, dsa_w_out, lru_w_in, lru_conv_w, lru_conv_b, lru_gate_a_w, lru_gate_a_b, lru_gate_x_w, lru_gate_x_b, lru_lambda, lru_w_out, rwkv_mu, rwkv_w_in, rwkv_w0, rwkv_w1, rwkv_w2, rwkv_a0, rwkv_a1, rwkv_a2, rwkv_k_k, rwkv_k_a, rwkv_r_k, rwkv_ln_w, rwkv_ln_b, rwkv_w_out, gla_w_in, gla_alpha_w2, gla_alpha_b, gla_norm_gain, gla_w_out):
    depth = mod_w.shape[0]
    bsz, _, d = x.shape
    mod = _modulation(c, mod_w, mod_b)
    for layer in range(depth):
        mixer, r = layer % 4, layer // 4
        mod3 = mod[layer].reshape(bsz, 1, 3 * d)
        gain = ln_gain[layer]
        if mixer == 0:
            x = _dsa_layer(x, gain, mod3, dsa_w_in[r], dsa_q_gain[r], dsa_k_gain[r], dsa_w_out[r])
        elif mixer == 1:
            x = _lru_layer(x, gain, mod3, lru_w_in[r], lru_conv_w[r], lru_conv_b[r], lru_gate_a_w[r],
                           lru_gate_a_b[r], lru_gate_x_w[r], lru_gate_x_b[r], lru_lambda[r], lru_w_out[r])
        elif mixer == 2:
            x = _rwkv_layer(x, gain, mod3, rwkv_mu[r], rwkv_w_in[r], rwkv_w0[r], rwkv_w1[r], rwkv_w2[r],
                            rwkv_a0[r], rwkv_a1[r], rwkv_a2[r], rwkv_k_k[r], rwkv_k_a[r],
                            rwkv_r_k[r].reshape(-1), rwkv_ln_w[r], rwkv_ln_b[r], rwkv_w_out[r])
        else:
            x = _gla_layer(x, gain, mod3, gla_w_in[r], gla_alpha_w2[r], gla_alpha_b[r],
                           gla_norm_gain[r], gla_w_out[r])
    return x
```

```python
import functools

import jax
import jax.numpy as jnp
from jax import lax
from jax.experimental import pallas as pl
from jax.experimental.pallas import tpu as pltpu

F32 = jnp.float32
BF16 = jnp.bfloat16
HIGHEST = lax.Precision.HIGHEST

LANES = 128
SUBLANES = 8
VMEM_LIMIT_BYTES = 56 * 1024 * 1024

RMS_EPS = 1e-6
ROPE_THETA = 10000.0

DSA_HEADS = 16
DSA_KV_HEADS = 4
DSA_HEAD_DIM = 64
DSA_IDX_HEADS = 8
DSA_IDX_DIM = 128
DSA_TOPK = 256
DSA_QBLOCK = 128
DSA_KEY_CHUNK = 256
LOG2E = 1.4426950408889634
DSA_IDX_SCALE = (DSA_IDX_HEADS * DSA_IDX_DIM) ** -0.5

LRU_BLOCKS = 16
LRU_CONV = 4
LRU_C = 8.0
LRU_GROUP = 256

RWKV_HEAD_DIM = 64
RWKV_GN_EPS = 64e-5
RWKV_CHUNK = 64

GLA_HEADS = 4
GLA_GATE_RANK = 16
GLA_GATE_NORM = 16.0
GLA_CHUNK = 64

NEG_BIG = -1e30
NT_DIMS = (((1,), (1,)), ((), ()))


def _params(*semantics):
    return pltpu.CompilerParams(dimension_semantics=semantics,
                                vmem_limit_bytes=VMEM_LIMIT_BYTES)


def _resident(shape):
    return pl.BlockSpec(shape, lambda *_: (0,) * len(shape), pipeline_mode=pl.Buffered(1))


def _dot(a, b):
    return jnp.dot(a.astype(BF16), b.astype(BF16), preferred_element_type=F32)


def _dot_nt(a, b):
    return lax.dot_general(a.astype(BF16), b.astype(BF16), NT_DIMS,
                           preferred_element_type=F32)


def _split3(x):
    hi = x.astype(BF16)
    r1 = x - hi.astype(F32)
    mid = r1.astype(BF16)
    lo = (r1 - mid.astype(F32)).astype(BF16)
    return hi, mid, lo


def _dot_exact_lhs(m01, x):
    hi, mid, lo = _split3(x)
    return (jnp.dot(m01, hi, preferred_element_type=F32)
            + jnp.dot(m01, mid, preferred_element_type=F32)
            + jnp.dot(m01, lo, preferred_element_type=F32))


def _dot_exact_rhs(x, m01):
    hi, mid, lo = _split3(x)
    return (jnp.dot(hi, m01, preferred_element_type=F32)
            + jnp.dot(mid, m01, preferred_element_type=F32)
            + jnp.dot(lo, m01, preferred_element_type=F32))


def _group_sum(z, g01):
    cols = [_dot_exact_rhs(z[:, c * LANES:(c + 1) * LANES], g01)
            for c in range(z.shape[1] // LANES)]
    return cols[0] if len(cols) == 1 else jnp.concatenate(cols, axis=1)


def _group_sum_lanes(z, lo):
    cols = []
    for c in range(z.shape[1] // LANES):
        zb = z[:, c * LANES:(c + 1) * LANES]
        s_lo = jnp.sum(jnp.where(lo, zb, 0.0), axis=1, keepdims=True)
        s_hi = jnp.sum(jnp.where(lo, 0.0, zb), axis=1, keepdims=True)
        cols.append(jnp.where(lo, s_lo, s_hi))
    return cols[0] if len(cols) == 1 else jnp.concatenate(cols, axis=1)


def _silu(x):
    return x * jax.nn.sigmoid(x)


def _softplus(z):
    return jnp.maximum(z, 0.0) + jnp.log1p(jnp.exp(-jnp.abs(z)))


def _prenorm(x, gain, scale, shift):
    ms = jnp.mean(x * x, axis=-1, keepdims=True)
    y = x * lax.rsqrt(ms + RMS_EPS) * gain
    return y * (1.0 + scale) + shift


def _gated_residual(y, g, x, gate, w_out):
    a = (y * _silu(g)).astype(BF16)
    return x + gate * jnp.dot(a, w_out, preferred_element_type=F32)


def _mod_kernel(c_ref, w_ref, b_ref, o_ref):
    o_ref[0] = jnp.dot(_silu(c_ref[...]), w_ref[0], precision=HIGHEST,
                       preferred_element_type=F32) + b_ref[0]


def _modulation(c, mod_w, mod_b):
    depth, d, _ = mod_w.shape
    bsz = c.shape[0]
    return pl.pallas_call(
        _mod_kernel,
        out_shape=jax.ShapeDtypeStruct((depth, bsz, 3 * d), F32),
        grid=(depth, 3),
        in_specs=[pl.BlockSpec((bsz, d), lambda l, j: (0, 0)),
                  pl.BlockSpec((1, d, d), lambda l, j: (l, 0, j)),
                  pl.BlockSpec((1, 1, d), lambda l, j: (l, 0, j))],
        out_specs=pl.BlockSpec((1, bsz, d), lambda l, j: (l, 0, j)),
        compiler_params=_params("arbitrary", "arbitrary"),
        name="adaln_mod",
    )(c, mod_w, mod_b.reshape(depth, 1, 3 * d))


def _mod_specs(d):
    return [pl.BlockSpec((1, 1, d), lambda b, t, j=j: (b, 0, j)) for j in range(3)]


def _head_group_ones():
    r = jnp.arange(LANES) // DSA_HEAD_DIM
    return (r[:, None] == r[None, :]).astype(BF16)


def _lower_tri_ones(n):
    return (jnp.arange(n)[:, None] >= jnp.arange(n)[None, :]).astype(BF16)


def _rope_tables(seq, dim, reps):
    half = dim // 2
    inv_freq = ROPE_THETA ** (-jnp.arange(half, dtype=F32) / half)
    ang = jnp.arange(seq, dtype=F32)[:, None] * inv_freq[None, :]
    cos = jnp.concatenate([jnp.cos(ang), jnp.cos(ang)], axis=1)
    sin = jnp.concatenate([-jnp.sin(ang), jnp.sin(ang)], axis=1)
    return jnp.tile(cos, (1, reps)), jnp.tile(sin, (1, reps))


def _rope64(x, cos, sin, lane_lo):
    cols = []
    for c in range(x.shape[1] // LANES):
        xb = x[:, c * LANES:(c + 1) * LANES]
        rot = jnp.where(lane_lo, pltpu.roll(xb, 96, 1), pltpu.roll(xb, 32, 1))
        cols.append(xb * cos + rot * sin)
    return cols[0] if len(cols) == 1 else jnp.concatenate(cols, axis=1)


def _rope128(x, cos, sin):
    cols = []
    for c in range(x.shape[1] // LANES):
        xb = x[:, c * LANES:(c + 1) * LANES]
        cols.append(xb * cos + pltpu.roll(xb, 64, 1) * sin)
    return cols[0] if len(cols) == 1 else jnp.concatenate(cols, axis=1)


_DSA_Q = DSA_HEADS * DSA_HEAD_DIM
_DSA_KV2 = DSA_KV_HEADS * LANES
_DSA_QI = DSA_IDX_HEADS * DSA_IDX_DIM
_DSA_OFF_Q = 0
_DSA_OFF_G = _DSA_OFF_Q + _DSA_Q
_DSA_OFF_QI = _DSA_OFF_G + _DSA_Q
_DSA_OFF_K = _DSA_OFF_QI + _DSA_QI
_DSA_OFF_V = _DSA_OFF_K + _DSA_KV2
_DSA_OFF_KI = _DSA_OFF_V + _DSA_KV2
_DSA_OFF_WI = _DSA_OFF_KI + DSA_IDX_DIM
_DSA_COLS = _DSA_OFF_WI + LANES


def _dsa_proj_kernel(x_ref, gain_ref, shift_ref, scale_ref, w_ref, qgain_ref, kgain_ref,
                     cos64_ref, sin64_ref, cos128_ref, sin128_ref, g01_ref,
                     q_ref, g_ref, qi_ref, k_ref, v_ref, ki_ref, wi_ref):
    h = _prenorm(x_ref[0], gain_ref[...], scale_ref[0], shift_ref[0]).astype(BF16)
    g01 = g01_ref[...]
    cos64, sin64 = cos64_ref[...], sin64_ref[...]
    cos128, sin128 = cos128_ref[...], sin128_ref[...]
    lane = lax.broadcasted_iota(jnp.int32, cos64.shape, 1)
    lane_lo = (lane % DSA_HEAD_DIM) < DSA_HEAD_DIM // 2

    def head_norm_rope(raw, gain):
        ms = _group_sum(raw * raw, g01) * (1.0 / DSA_HEAD_DIM)
        return _rope64(raw * lax.rsqrt(ms + RMS_EPS) * gain, cos64, sin64, lane_lo)

    def col(off, n):
        return jnp.dot(h, w_ref[:, off:off + n], preferred_element_type=F32)

    tm = h.shape[0]
    row_lo64 = lax.broadcasted_iota(jnp.int32, (LANES, tm), 0) < DSA_HEAD_DIM
    q = head_norm_rope(col(_DSA_OFF_Q, _DSA_Q), qgain_ref[...]) * (LOG2E * DSA_HEAD_DIM ** -0.5)
    qt = q.T
    for hh in range(DSA_HEADS):
        blk = qt[(hh // 2) * LANES:(hh // 2 + 1) * LANES, :]
        keep = row_lo64 if hh % 2 == 0 else ~row_lo64
        q_ref[0, hh] = jnp.where(keep, blk, 0.0).astype(BF16)
    g_ref[0] = col(_DSA_OFF_G, _DSA_Q).astype(BF16)
    qit = _rope128(col(_DSA_OFF_QI, _DSA_QI), cos128, sin128).T.astype(BF16)
    for hh in range(DSA_IDX_HEADS):
        qi_ref[0, hh] = qit[hh * DSA_IDX_DIM:(hh + 1) * DSA_IDX_DIM, :]
    k2 = head_norm_rope(col(_DSA_OFF_K, _DSA_KV2), kgain_ref[...]).astype(BF16)
    for kv in range(DSA_KV_HEADS):
        k_ref[0, kv] = k2[:, kv * LANES:(kv + 1) * LANES]
    vt = col(_DSA_OFF_V, _DSA_KV2).T
    for kv in range(DSA_KV_HEADS):
        v_ref[0, kv] = jnp.where(row_lo64, vt[kv * LANES:(kv + 1) * LANES, :], 1.0).astype(BF16)
    ki_ref[0] = _rope128(col(_DSA_OFF_KI, DSA_IDX_DIM), cos128, sin128).astype(BF16)
    wit = (col(_DSA_OFF_WI, LANES) * DSA_IDX_SCALE).T
    wi_ref[0] = wit[0:DSA_IDX_HEADS, :]


def _dsa_weights(w_in):
    d = w_in.shape[0]
    q_end = _DSA_Q
    k_end = q_end + DSA_KV_HEADS * DSA_HEAD_DIM
    v_end = k_end + DSA_KV_HEADS * DSA_HEAD_DIM
    g_end = v_end + _DSA_Q
    qi_end = g_end + _DSA_QI
    wi_end = qi_end + DSA_IDX_HEADS
    wq, wk, wv, wg = w_in[:, :q_end], w_in[:, q_end:k_end], w_in[:, k_end:v_end], w_in[:, v_end:g_end]
    wqi, wwi, wki = w_in[:, g_end:qi_end], w_in[:, qi_end:wi_end], w_in[:, wi_end:]

    def dup(w):
        w = w.reshape(d, DSA_KV_HEADS, 1, DSA_HEAD_DIM)
        return jnp.broadcast_to(w, (d, DSA_KV_HEADS, 2, DSA_HEAD_DIM)).reshape(d, _DSA_KV2)

    wwi = jnp.pad(wwi, ((0, 0), (0, LANES - DSA_IDX_HEADS)))
    return jnp.concatenate([wq, wg, wqi, dup(wk), dup(wv), wki, wwi], axis=1).astype(BF16)


def _dsa_project(x, gain, mod3, w_in, q_gain, k_gain, tm=512):
    bsz, seq, d = x.shape
    tm = min(tm, seq)
    w = _dsa_weights(w_in)
    cos64, sin64 = _rope_tables(seq, DSA_HEAD_DIM, LANES // DSA_HEAD_DIM)
    cos128, sin128 = _rope_tables(seq, DSA_IDX_DIM, 1)
    shift_spec, scale_spec, _ = _mod_specs(d)
    table = pl.BlockSpec((tm, LANES), lambda b, t: (t, 0))
    row = lambda n, dt: jax.ShapeDtypeStruct((bsz, seq, n), dt)
    heads = lambda n: jax.ShapeDtypeStruct((bsz, n, seq, LANES), BF16)
    head_spec = lambda n: pl.BlockSpec((1, n, tm, LANES), lambda b, t: (b, 0, t, 0))
    heads_t = lambda n: jax.ShapeDtypeStruct((bsz, n, LANES, seq), BF16)
    head_t_spec = lambda n: pl.BlockSpec((1, n, LANES, tm), lambda b, t: (b, 0, 0, t))
    return pl.pallas_call(
        _dsa_proj_kernel,
        out_shape=[heads_t(DSA_HEADS), row(_DSA_Q, BF16), heads_t(DSA_IDX_HEADS),
                   heads(DSA_KV_HEADS), heads_t(DSA_KV_HEADS), row(DSA_IDX_DIM, BF16),
                   jax.ShapeDtypeStruct((bsz, DSA_IDX_HEADS, seq), F32)],
        grid=(bsz, seq // tm),
        in_specs=[pl.BlockSpec((1, tm, d), lambda b, t: (b, t, 0)),
                  _resident((1, d)), shift_spec, scale_spec, _resident(w.shape),
                  _resident((1, _DSA_Q)), _resident((1, _DSA_KV2)),
                  table, table, table, table, _resident((LANES, LANES))],
        out_specs=[head_t_spec(DSA_HEADS),
                   pl.BlockSpec((1, tm, _DSA_Q), lambda b, t: (b, t, 0)),
                   head_t_spec(DSA_IDX_HEADS),
                   head_spec(DSA_KV_HEADS),
                   head_t_spec(DSA_KV_HEADS),
                   pl.BlockSpec((1, tm, DSA_IDX_DIM), lambda b, t: (b, t, 0)),
                   pl.BlockSpec((1, DSA_IDX_HEADS, tm), lambda b, t: (b, 0, t))],
        compiler_params=_params("parallel", "parallel"),
        name="dsa_proj",
    )(x, gain.reshape(1, d), mod3, mod3, w,
      jnp.tile(q_gain, DSA_HEADS).reshape(1, _DSA_Q),
      jnp.tile(k_gain, _DSA_KV2 // DSA_HEAD_DIM).reshape(1, _DSA_KV2),
      cos64, sin64, cos128, sin128, _head_group_ones())


def _reduce_rows(x, op):
    part = op(x.reshape(x.shape[0] // 64, 64, x.shape[1]), axis=0)
    return op(part, axis=0, keepdims=True)


def _sortable_to_float(key):
    bits = jnp.where(key >= 0, key, key ^ jnp.int32(0x7FFFFFFF))
    return lax.bitcast_convert_type(bits, F32)


def _dsa_attn_kernel(n_sel, chunk, qi_ref, wi_ref, ki_ref, q_ref, k_ref, v_ref, tri_ref,
                     g_ref, x_ref, gate_ref, wout_ref, o_ref, score_ref, bias_ref, qk_ref):
    seq, qb = score_ref.shape
    per = chunk // qb
    needed = lax.div(pl.program_id(1) + per, per)
    for j in range(seq // chunk):
        width = chunk * (j + 1)
        block = _dsa_attn_block_all if width <= n_sel else _dsa_attn_block
        pl.when(needed == j + 1)(functools.partial(
            block, n_sel, width, qi_ref, wi_ref, ki_ref, q_ref, k_ref, v_ref,
            tri_ref, g_ref, x_ref, gate_ref, wout_ref, o_ref, score_ref, bias_ref, qk_ref))


def _dsa_attn_block(n_sel, width, qi_ref, wi_ref, ki_ref, q_ref, k_ref, v_ref, tri_ref,
                    g_ref, x_ref, gate_ref, wout_ref, o_ref, score_ref, bias_ref, qk_ref):
    blk = pl.program_id(1)
    qb = score_ref.shape[1]
    ki = ki_ref[0, 0:width, :]
    wi = wi_ref[0]
    stacked = DSA_HEADS // DSA_KV_HEADS

    def heads_of(ref, first):
        return jnp.concatenate([ref[0, first + j] for j in range(stacked)], axis=1)

    acc = None
    for grp in range(DSA_IDX_HEADS // stacked):
        logits = jnp.dot(ki, heads_of(qi_ref, grp * stacked), preferred_element_type=F32)
        for j in range(stacked):
            hh = grp * stacked + j
            term = wi[hh:hh + 1, :] * jnp.maximum(logits[:, j * qb:(j + 1) * qb], 0.0)
            acc = term if acc is None else acc + term
    q_pos = blk * qb + lax.broadcasted_iota(jnp.int32, (1, qb), 1)
    key_pos = lax.broadcasted_iota(jnp.int32, (width, 1), 0)
    score_ref[0:width, :] = jnp.where(key_pos <= q_pos, acc, -jnp.inf)

    k_sel = jnp.float32(n_sel)

    def count_ge(thr):
        return _reduce_rows(jnp.where(score_ref[0:width, :] >= thr, 1.0, 0.0), jnp.sum)

    def reaches(cand):
        return jnp.where(count_ge(_sortable_to_float(cand)) >= k_sel, 1, 0)

    def q_heads(kv):
        return jnp.concatenate([q_ref[0, kv * stacked + j] for j in range(stacked)], axis=1)

    half = width // 2
    per_step = 4

    def search_step(i, key):
        kv = i // 2
        rows = pl.ds(pl.multiple_of((i % 2) * half, half), half)
        qk_ref[kv, rows, :] = jnp.dot(k_ref[0, kv, rows, :], q_heads(kv),
                                      preferred_element_type=F32)
        for j in range(per_step):
            n = i * per_step + j
            cand = jnp.where(n == 0, jnp.int32(0),
                             key + lax.shift_left(jnp.int32(1), jnp.int32(31) - n))
            key = jnp.where(reaches(cand) > 0, cand, key)
        return key

    key0 = jnp.full((1, qb), -2 ** 31, jnp.int32)
    thr = _sortable_to_float(lax.fori_loop(0, 2 * DSA_KV_HEADS, search_step, key0))

    score = score_ref[0:width, :]
    gt = score > thr
    need = k_sel - _reduce_rows(jnp.where(gt, 1.0, 0.0), jnp.sum)
    take_all = q_pos < n_sel
    tri = tri_ref[...]
    run = jnp.zeros((1, qb), F32)
    for c in range(width // LANES):
        sl = slice(c * LANES, (c + 1) * LANES)
        eq = jnp.where(score[sl, :] == thr, 1.0, 0.0)
        incl = jnp.dot(tri, eq.astype(BF16), preferred_element_type=F32)
        tie_ok = (incl - eq + run) < need
        run = run + incl[LANES - 1:LANES, :]
        sel = gt[sl, :] | ((eq > 0.0) & tie_ok) | take_all
        causal = key_pos[sl, :] <= q_pos
        bias_ref[sl, :] = jnp.where(sel & causal, 0.0, NEG_BIG)

    _dsa_attn_tail(width, v_ref, g_ref, x_ref, gate_ref, wout_ref, o_ref, bias_ref, qk_ref)


def _dsa_attn_block_all(n_sel, width, qi_ref, wi_ref, ki_ref, q_ref, k_ref, v_ref, tri_ref,
                        g_ref, x_ref, gate_ref, wout_ref, o_ref, score_ref, bias_ref, qk_ref):
    qb = bias_ref.shape[1]
    stacked = DSA_HEADS // DSA_KV_HEADS
    q_pos = pl.program_id(1) * qb + lax.broadcasted_iota(jnp.int32, (1, qb), 1)
    key_pos = lax.broadcasted_iota(jnp.int32, (width, 1), 0)
    bias_ref[0:width, :] = jnp.where(key_pos <= q_pos, 0.0, NEG_BIG)
    for kv in range(DSA_KV_HEADS):
        q4 = jnp.concatenate([q_ref[0, kv * stacked + j] for j in range(stacked)], axis=1)
        qk_ref[kv, 0:width, :] = jnp.dot(k_ref[0, kv, 0:width, :], q4, preferred_element_type=F32)
    _dsa_attn_tail(width, v_ref, g_ref, x_ref, gate_ref, wout_ref, o_ref, bias_ref, qk_ref)


def _dsa_attn_tail(width, v_ref, g_ref, x_ref, gate_ref, wout_ref, o_ref, bias_ref, qk_ref):
    qb = bias_ref.shape[1]
    stacked = DSA_HEADS // DSA_KV_HEADS
    bias = bias_ref[0:width, :]
    outs = []
    for kv in range(DSA_KV_HEADS):
        s4 = qk_ref[kv, 0:width, :]
        ps = []
        for j in range(stacked):
            s = s4[:, j * qb:(j + 1) * qb] + bias
            ps.append(jnp.exp2(s - _reduce_rows(s, jnp.max)).astype(BF16))
        ov = jnp.dot(v_ref[0, kv, :, 0:width], jnp.concatenate(ps, axis=1),
                     preferred_element_type=F32)
        ov = ov[0:DSA_HEAD_DIM, :] / ov[DSA_HEAD_DIM:DSA_HEAD_DIM + 1, :]
        for c in range(stacked // 2):
            even = ov[:, (2 * c) * qb:(2 * c + 1) * qb]
            odd = ov[:, (2 * c + 1) * qb:(2 * c + 2) * qb]
            outs.append(jnp.concatenate([even, odd], axis=0).T)
    o_ref[0] = _gated_residual(jnp.concatenate(outs, axis=1), g_ref[0].astype(F32), x_ref[0],
                               gate_ref[0], wout_ref[...])


def _dsa_attention(q, qi, k2, vt, ki, wi, g, x, mod3, w_out):
    bsz, seq, d = x.shape
    qb = DSA_QBLOCK
    n_sel = min(DSA_TOPK, seq // 4)
    chunk = min(DSA_KEY_CHUNK, seq // 2)
    return pl.pallas_call(
        functools.partial(_dsa_attn_kernel, n_sel, chunk),
        out_shape=jax.ShapeDtypeStruct((bsz, seq, d), F32),
        grid=(bsz, seq // qb),
        in_specs=[pl.BlockSpec((1, DSA_IDX_HEADS, LANES, qb), lambda b, i: (b, 0, 0, i)),
                  pl.BlockSpec((1, DSA_IDX_HEADS, qb), lambda b, i: (b, 0, i)),
                  pl.BlockSpec((1, seq, DSA_IDX_DIM), lambda b, i: (b, 0, 0)),
                  pl.BlockSpec((1, DSA_HEADS, LANES, qb), lambda b, i: (b, 0, 0, i)),
                  pl.BlockSpec((1, DSA_KV_HEADS, seq, LANES), lambda b, i: (b, 0, 0, 0)),
                  pl.BlockSpec((1, DSA_KV_HEADS, LANES, seq), lambda b, i: (b, 0, 0, 0)),
                  _resident((LANES, LANES)),
                  pl.BlockSpec((1, qb, _DSA_Q), lambda b, i: (b, i, 0)),
                  pl.BlockSpec((1, qb, d), lambda b, i: (b, i, 0)),
                  _mod_specs(d)[2],
                  _resident((_DSA_Q, d))],
        out_specs=pl.BlockSpec((1, qb, d), lambda b, i: (b, i, 0)),
        scratch_shapes=[pltpu.VMEM((seq, qb), F32), pltpu.VMEM((seq, qb), F32),
                        pltpu.VMEM((DSA_KV_HEADS, seq, (DSA_HEADS // DSA_KV_HEADS) * qb), F32)],
        compiler_params=_params("parallel", "parallel"),
        name="dsa_attn",
    )(qi, wi, ki, q, k2, vt, _lower_tri_ones(LANES), g, x, mod3, w_out.astype(BF16))


def _dsa_layer(x, gain, mod3, w_in, q_gain, k_gain, w_out):
    q, g, qi, k2, vt, ki, wi = _dsa_project(x, gain, mod3, w_in, q_gain, k_gain)
    return _dsa_attention(q, qi, k2, vt, ki, wi, g, x, mod3, w_out)


def _lru_kernel(x_ref, gain_ref, shift_ref, scale_ref, gate_ref, win_ref, cw_ref, cb_ref,
                wa_ref, ba_ref, wx_ref, bx_ref, lam_ref, wout_ref, o_ref,
                h_buf, ubuf, a_buf, b_buf, hs_buf, g_buf, h_carry):
    tt = x_ref.shape[1]
    width = ubuf.shape[1]
    halo = SUBLANES

    @pl.when(pl.program_id(1) == 0)
    def _():
        ubuf[0:halo, :] = jnp.zeros((halo, width), F32)
        h_carry[...] = jnp.zeros_like(h_carry)

    h_buf[...] = _prenorm(x_ref[0], gain_ref[...], scale_ref[0], shift_ref[0]).astype(BF16)
    ubuf[halo:halo + tt, :] = jnp.dot(h_buf[...], win_ref[:, 0:width], preferred_element_type=F32)
    cw = cw_ref[...]
    u = cb_ref[...]
    for j in range(LRU_CONV):
        start = halo - (LRU_CONV - 1) + j
        u = u + cw[j:j + 1, :] * ubuf[start:start + tt, :]
    ubuf[0:halo, :] = ubuf[tt:tt + halo, :]

    g_buf[...] = jnp.dot(h_buf[...], win_ref[:, width:2 * width], preferred_element_type=F32)

    sp = _softplus(-lam_ref[...])
    for c in range(width // LRU_GROUP):
        sl = slice(c * LRU_GROUP, (c + 1) * LRU_GROUP)
        uc = u[:, sl]
        ub = uc.astype(BF16)
        r = jax.nn.sigmoid(jnp.dot(ub, wa_ref[c], preferred_element_type=F32) + ba_ref[:, sl])
        i = jax.nn.sigmoid(jnp.dot(ub, wx_ref[c], preferred_element_type=F32) + bx_ref[:, sl])
        a = jnp.exp(-LRU_C * r * sp[:, sl])
        a_buf[:, sl] = a
        b_buf[:, sl] = jnp.sqrt(1.0 - a * a) * (i * uc)

    row = lax.broadcasted_iota(jnp.int32, (SUBLANES, width), 0)

    def group(gi, h_prev):
        r0 = pl.multiple_of(gi * SUBLANES, SUBLANES)
        a = a_buf[pl.ds(r0, SUBLANES), :]
        b = b_buf[pl.ds(r0, SUBLANES), :]
        for s in (1, 2, 4):
            ok = row >= s
            b = jnp.where(ok, a * pltpu.roll(b, s, 0) + b, b)
            a = jnp.where(ok, a * pltpu.roll(a, s, 0), a)
        h = a * h_prev + b
        hs_buf[pl.ds(r0, SUBLANES), :] = h
        return jnp.broadcast_to(h[SUBLANES - 1:SUBLANES, :], (SUBLANES, width))

    h_carry[...] = lax.fori_loop(0, tt // SUBLANES, group, h_carry[...])
    o_ref[0] = _gated_residual(hs_buf[...], g_buf[...], x_ref[0], gate_ref[0], wout_ref[...])


def _lru_gate_blocks(w):
    nb, bd, _ = w.shape
    per = LRU_GROUP // bd
    w = w.reshape(nb // per, per, bd, bd)
    eye = jnp.eye(per, dtype=w.dtype)
    return jnp.einsum('gpcd,pq->gpcqd', w, eye).reshape(nb // per, LRU_GROUP, LRU_GROUP).astype(BF16)


def _lru_layer(x, gain, mod3, w_in, conv_w, conv_b, gate_a_w, gate_a_b, gate_x_w, gate_x_b, lam,
               w_out, tt=512):
    bsz, seq, d = x.shape
    tt = min(tt, seq)
    width = w_in.shape[1] // 2
    ngroups = width // LRU_GROUP
    tile = pl.BlockSpec((1, tt, d), lambda b, t: (b, t, 0))
    vec = lambda: _resident((1, width))
    gatew = lambda: _resident((ngroups, LRU_GROUP, LRU_GROUP))
    cw = jnp.pad(conv_w, ((0, SUBLANES - LRU_CONV), (0, 0)))
    return pl.pallas_call(
        _lru_kernel,
        out_shape=jax.ShapeDtypeStruct((bsz, seq, d), F32),
        grid=(bsz, seq // tt),
        in_specs=[tile, _resident((1, d)), *_mod_specs(d), _resident((d, 2 * width)),
                  _resident((SUBLANES, width)), vec(), gatew(), vec(), gatew(), vec(), vec(),
                  _resident((width, d))],
        out_specs=tile,
        scratch_shapes=[pltpu.VMEM((tt, d), BF16),
                        pltpu.VMEM((tt + SUBLANES, width), F32),
                        pltpu.VMEM((tt, width), F32), pltpu.VMEM((tt, width), F32),
                        pltpu.VMEM((tt, width), F32), pltpu.VMEM((tt, width), F32),
                        pltpu.VMEM((SUBLANES, width), F32)],
        compiler_params=_params("parallel", "arbitrary"),
        name="rglru_layer",
    )(x, gain.reshape(1, d), mod3, mod3, mod3, w_in.astype(BF16), cw, conv_b.reshape(1, width),
      _lru_gate_blocks(gate_a_w), gate_a_b.reshape(1, width),
      _lru_gate_blocks(gate_x_w), gate_x_b.reshape(1, width), lam.reshape(1, width),
      w_out.astype(BF16))


def _rwkv_kernel(x_ref, gain_ref, shift_ref, scale_ref, gate_ref, mu_ref, w_ref, w1_ref, w2_ref,
                 a1_ref, a2_ref, w0_ref, a0_ref, kkw_ref, ka_ref, rk_ref, lnw_ref, lnb_ref,
                 ltri_ref, wout_ref, o_ref,
                 r_s, lw_s, k_s, v_s, kk_s, a_s, bonus_s, g_s, y_s, carry, state):
    tm = x_ref.shape[1]
    d = x_ref.shape[2]
    cs = RWKV_CHUNK
    hd = RWKV_HEAD_DIM

    @pl.when(pl.program_id(1) == 0)
    def _():
        carry[...] = jnp.zeros_like(carry)
        state[...] = jnp.zeros_like(state)

    lo_t = lax.broadcasted_iota(jnp.int32, (tm, LANES), 1) < hd
    h = _prenorm(x_ref[0], gain_ref[...], scale_ref[0], shift_ref[0])
    first = lax.broadcasted_iota(jnp.int32, (tm, d), 0) == 0
    h_prev = jnp.where(first, carry[0:1, :], pltpu.roll(h, 1, 0))
    carry[...] = jnp.broadcast_to(h[tm - 1:tm, :], carry.shape)
    delta = h_prev - h
    mu = mu_ref[...]

    def mix(n):
        return (h + delta * mu[n:n + 1, :]).astype(BF16)

    r = jnp.dot(mix(0), w_ref[0], preferred_element_type=F32)
    k = jnp.dot(mix(1), w_ref[1], preferred_element_type=F32)
    v = jnp.dot(mix(2), w_ref[2], preferred_element_type=F32)
    g_s[...] = jnp.dot(mix(3), w_ref[3], preferred_element_type=F32)
    w_lora = _dot(jnp.tanh(jnp.dot(mix(4), w1_ref[...], preferred_element_type=F32)), w2_ref[...])
    w_log = -_softplus(-(w0_ref[...] + w_lora)) - 0.5
    lw_s[...] = -jnp.exp(w_log)
    a_lora = _dot(jnp.dot(mix(5), a1_ref[...], preferred_element_type=F32), a2_ref[...])
    a = jax.nn.sigmoid(a0_ref[...] + a_lora)
    kk = k * kkw_ref[...]
    kk = kk / jnp.maximum(jnp.sqrt(_group_sum_lanes(kk * kk, lo_t)), 1e-12)
    k = k * (1.0 + (a - 1.0) * ka_ref[...])
    r_s[...] = r
    k_s[...] = k
    v_s[...] = v
    kk_s[...] = kk
    a_s[...] = a
    bonus_s[...] = _group_sum_lanes(r * k * rk_ref[...], lo_t) * v

    lane = lax.broadcasted_iota(jnp.int32, (cs, LANES), 1)
    lo = lane < hd
    ri = lax.broadcasted_iota(jnp.int32, (LANES, LANES), 0)
    ci = lax.broadcasted_iota(jnp.int32, (LANES, LANES), 1)
    same = (ri < cs) == (ci < cs)
    strict = same & (ci < ri)
    incl = same & (ci <= ri)
    eye = jnp.where(ri == ci, 1.0, 0.0)
    ltri = ltri_ref[...]

    def stack(z):
        return jnp.concatenate([z, z], axis=0)

    def split_heads(z):
        return jnp.concatenate([jnp.where(lo, z, 0.0), jnp.where(lo, 0.0, z)], axis=0)

    def own(z):
        return jnp.where(lo, z[0:cs], z[cs:2 * cs])

    pairs = range(d // LANES)
    lanes = [slice(p * LANES, (p + 1) * LANES) for p in pairs]

    def chunk(c, _):
        rows = pl.ds(pl.multiple_of(c * cs, cs), cs)
        lw = lw_s[rows, :]
        g_inc = _dot_exact_lhs(ltri, lw)
        g_last = g_inc[cs - 1:cs, :]
        kk = kk_s[rows, :]
        kc = k_s[rows, :]
        vc = v_s[rows, :]
        bvec = kk * a_s[rows, :]
        e_neg = jnp.exp(-g_inc)
        e_end = jnp.exp(g_last - g_inc)
        at = -kk * jnp.exp(g_inc - lw)
        rt = r_s[rows, :] * jnp.exp(g_inc)
        bt = bvec * e_neg
        kt = kc * e_neg
        bend = (bvec * e_end).astype(BF16)
        kend = (kc * e_end).astype(BF16)
        decay = jnp.exp(g_last)

        lhs = [jnp.concatenate([at[:, s], rt[:, s]], axis=0).astype(BF16) for s in lanes]
        nbk = [_dot_nt(lhs[p], jnp.concatenate([split_heads(bt[:, lanes[p]]),
                                                split_heads(kt[:, lanes[p]])], axis=0))
               for p in pairs]
        n_ab = [jnp.where(strict, stack(nbk[p][0:cs, 0:LANES]), 0.0) for p in pairs]
        n_rb = [jnp.where(incl, stack(nbk[p][cs:2 * cs, 0:LANES]), 0.0).astype(BF16) for p in pairs]
        n_ak = [jnp.where(strict, stack(nbk[p][0:cs, LANES:2 * LANES]), 0.0) for p in pairs]
        n_rk = [jnp.where(incl, stack(nbk[p][cs:2 * cs, LANES:2 * LANES]), 0.0) for p in pairs]

        inv = [eye + n_ab[p] for p in pairs]
        pw = [_dot(n_ab[p], n_ab[p]) for p in pairs]
        for _stage in range(4):
            res = [_dot(pw[p], jnp.concatenate([pw[p], inv[p]], axis=1)) for p in pairs]
            pw = [res[p][:, 0:LANES] for p in pairs]
            inv = [inv[p] + res[p][:, LANES:2 * LANES] for p in pairs]
        inv = [(inv[p] + _dot(pw[p], inv[p])).astype(BF16) for p in pairs]

        vv = [stack(vc[:, s]).astype(BF16) for s in lanes]
        nv = [_dot(jnp.concatenate([n_ak[p], n_rk[p]], axis=0), vv[p]) for p in pairs]
        w1 = [own(nv[p][0:LANES]) for p in pairs]
        tz = [_dot(inv[p], jnp.concatenate([stack(at[:, lanes[p]]), stack(w1[p])], axis=1))
              for p in pairs]
        a2 = [own(tz[p][:, 0:LANES]) for p in pairs]
        u0 = [own(tz[p][:, LANES:2 * LANES]) for p in pairs]
        rz = [_dot(n_rb[p], jnp.concatenate([stack(a2[p]), stack(u0[p])], axis=1)) for p in pairs]
        r2 = [rt[:, lanes[p]] + own(rz[p][:, 0:LANES]) for p in pairs]
        y0 = [own(rz[p][:, LANES:2 * LANES]) + own(nv[p][LANES:2 * LANES]) for p in pairs]
        mlr = [jnp.where(same, _dot(a2[p].T, bend[:, lanes[p]]), 0.0) for p in pairs]
        c0 = [jnp.where(same, _dot(jnp.concatenate([u0[p], vc[:, lanes[p]]], axis=0).T,
                                   jnp.concatenate([bend[:, lanes[p]], kend[:, lanes[p]]], axis=0)), 0.0)
              for p in pairs]

        ys = []
        for p in pairs:
            s_bd = state[p]
            ys.append(_dot_nt(r2[p], s_bd) + y0[p])
            state[p] = s_bd * decay[:, lanes[p]] + _dot(s_bd, mlr[p]) + c0[p]
        y = jnp.concatenate(ys, axis=1)
        mean = _group_sum_lanes(y, lo) * (1.0 / hd)
        yc = y - mean
        var = _group_sum_lanes(yc * yc, lo) * (1.0 / hd)
        y_s[rows, :] = (yc * lax.rsqrt(var + RWKV_GN_EPS) * lnw_ref[...] + lnb_ref[...]
                        + bonus_s[rows, :])
        return 0

    lax.fori_loop(0, tm // cs, chunk, 0, unroll=2)
    o_ref[0] = _gated_residual(y_s[...], g_s[...], x_ref[0], gate_ref[0], wout_ref[...])


def _rwkv_layer(x, gain, mod3, mu, w_in, w0, w1, w2, a0, a1, a2, k_k, k_a, r_k, ln_w, ln_b, w_out,
                tm=512):
    bsz, seq, d = x.shape
    tm = min(tm, seq)
    rank = w1.shape[1]
    pad_c = lambda w: jnp.pad(w, ((0, 0), (0, LANES - rank))).astype(BF16)
    pad_r = lambda w: jnp.pad(w, ((0, LANES - rank), (0, 0))).astype(BF16)
    vec = lambda p: p.reshape(1, d)
    tile = pl.BlockSpec((1, tm, d), lambda b, t: (b, t, 0))
    buf = pltpu.VMEM((tm, d), F32)
    return pl.pallas_call(
        _rwkv_kernel,
        out_shape=jax.ShapeDtypeStruct((bsz, seq, d), F32),
        grid=(bsz, seq // tm),
        in_specs=[tile, _resident((1, d)), *_mod_specs(d), _resident((SUBLANES, d)),
                  _resident((4, d, d)), _resident((d, LANES)), _resident((LANES, d)),
                  _resident((d, LANES)), _resident((LANES, d))]
                 + [_resident((1, d))] * 7
                 + [_resident((RWKV_CHUNK, RWKV_CHUNK)), _resident((d, d))],
        out_specs=tile,
        scratch_shapes=[buf] * 9 + [pltpu.VMEM((SUBLANES, d), F32),
                                    pltpu.VMEM((d // LANES, LANES, LANES), F32)],
        compiler_params=_params("parallel", "arbitrary"),
        name="rwkv_layer",
    )(x, gain.reshape(1, d), mod3, mod3, mod3,
      jnp.pad(mu, ((0, SUBLANES - mu.shape[0]), (0, 0))),
      w_in.astype(BF16), pad_c(w1), pad_r(w2), pad_c(a1), pad_r(a2), vec(w0), vec(a0),
      vec(k_k), vec(k_a), vec(r_k), vec(ln_w), vec(ln_b),
      _lower_tri_ones(RWKV_CHUNK), w_out.astype(BF16))


def _gla_kernel(x_ref, gain_ref, shift_ref, scale_ref, gate_ref, win_ref, w2_ref, ab_ref,
                ngain_ref, ltri_ref, wout_ref, o_ref, o_buf, state):
    tt = x_ref.shape[1]
    key_dim = w2_ref.shape[1]
    val_dim = wout_ref.shape[0]
    dk = key_dim // GLA_HEADS
    dv = val_dim // GLA_HEADS
    cs = GLA_CHUNK

    @pl.when(pl.program_id(1) == 0)
    def _():
        state[...] = jnp.zeros_like(state)

    h = _prenorm(x_ref[0], gain_ref[...], scale_ref[0], shift_ref[0]).astype(BF16)

    def col(off, n):
        return jnp.dot(h, win_ref[:, off:off + n], preferred_element_type=F32)

    q = col(0, key_dim)
    k = col(key_dim, key_dim)
    v = col(2 * key_dim, val_dim)
    g = col(2 * key_dim + val_dim, val_dim)
    a_low = col(2 * key_dim + 2 * val_dim, LANES)

    ri = lax.broadcasted_iota(jnp.int32, (cs, cs), 0)
    ci = lax.broadcasted_iota(jnp.int32, (cs, cs), 1)
    causal = ci <= ri
    ltri = ltri_ref[...]
    z = _dot(a_low, w2_ref[...]) + ab_ref[...]
    log_alpha = -_softplus(-z) * (1.0 / GLA_GATE_NORM)

    for c in range(tt // cs):
        rows = slice(c * cs, (c + 1) * cs)
        cum = _dot_exact_lhs(ltri, log_alpha[rows, :])
        last = cum[cs - 1:cs, :]
        kc = k[rows, :]
        q_dec = q[rows, :] * (dk ** -0.5) * jnp.exp(cum)
        k_inv = kc * jnp.exp(-cum)
        k_end = kc * jnp.exp(last - cum)
        decay = jnp.exp(last)
        for hh in range(GLA_HEADS):
            ks = slice(hh * dk, (hh + 1) * dk)
            vs = slice(hh * dv, (hh + 1) * dv)
            vh = v[rows, vs]
            st = state[hh]
            att = jnp.where(causal, _dot_nt(q_dec[:, ks], k_inv[:, ks]), 0.0)
            o = _dot(att, vh) + _dot_nt(q_dec[:, ks], st)
            state[hh] = st * decay[:, ks] + _dot(vh.T, k_end[:, ks])
            ms = jnp.mean(o * o, axis=-1, keepdims=True)
            o_buf[rows, vs] = o * lax.rsqrt(ms + RMS_EPS) * ngain_ref[...]

    o_ref[0] = _gated_residual(o_buf[...], g, x_ref[0], gate_ref[0], wout_ref[...])


def _gla_layer(x, gain, mod3, w_in, alpha_w2, alpha_b, norm_gain, w_out, tt=512):
    bsz, seq, d = x.shape
    tt = min(tt, seq)
    key_dim, val_dim = alpha_w2.shape[1], w_out.shape[0]
    dk, dv = key_dim // GLA_HEADS, val_dim // GLA_HEADS
    w = jnp.pad(w_in, ((0, 0), (0, LANES - GLA_GATE_RANK))).astype(BF16)
    w2 = jnp.pad(alpha_w2, ((0, LANES - GLA_GATE_RANK), (0, 0))).astype(BF16)
    tile = pl.BlockSpec((1, tt, d), lambda b, t: (b, t, 0))
    return pl.pallas_call(
        _gla_kernel,
        out_shape=jax.ShapeDtypeStruct((bsz, seq, d), F32),
        grid=(bsz, seq // tt),
        in_specs=[tile, _resident((1, d)), *_mod_specs(d), _resident(w.shape),
                  _resident((LANES, key_dim)), _resident((1, key_dim)), _resident((1, dv)),
                  _resident((GLA_CHUNK, GLA_CHUNK)), _resident((val_dim, d))],
        out_specs=tile,
        scratch_shapes=[pltpu.VMEM((tt, val_dim), F32), pltpu.VMEM((GLA_HEADS, dv, dk), F32)],
        compiler_params=_params("parallel", "arbitrary"),
        name="gla_layer",
    )(x, gain.reshape(1, d), mod3, mod3, mod3, w, w2, alpha_b.reshape(1, key_dim),
      norm_gain.reshape(1, dv), _lower_tri_ones(GLA_CHUNK), w_out.astype(BF16))


def kernel(x, c, ln_gain, mod_w, mod_b, dsa_w_in, dsa_q_gain, dsa_k_gain, dsa_w_out, lru_w_in, lru_conv_w, lru_conv_b, lru_gate_a_w, lru_gate_a_b, lru_gate_x_w, lru_gate_x_b, lru_lambda, lru_w_out, rwkv_mu, rwkv_w_in, rwkv_w0, rwkv_w1, rwkv_w2, rwkv_a0, rwkv_a1, rwkv_a2, rwkv_k_k, rwkv_k_a, rwkv_r_k, rwkv_ln_w, rwkv_ln_b, rwkv_w_out, gla_w_in, gla_alpha_w2, gla_alpha_b, gla_norm_gain, gla_w_out):
    depth = mod_w.shape[0]
    bsz, _, d = x.shape
    mod = _modulation(c, mod_w, mod_b)
    for layer in range(depth):
        mixer, r = layer % 4, layer // 4
        mod3 = mod[layer].reshape(bsz, 1, 3 * d)
        gain = ln_gain[layer]
        if mixer == 0:
            x = _dsa_layer(x, gain, mod3, dsa_w_in[r], dsa_q_gain[r], dsa_k_gain[r], dsa_w_out[r])
        elif mixer == 1:
            x = _lru_layer(x, gain, mod3, lru_w_in[r], lru_conv_w[r], lru_conv_b[r], lru_gate_a_w[r],
                           lru_gate_a_b[r], lru_gate_x_w[r], lru_gate_x_b[r], lru_lambda[r], lru_w_out[r])
        elif mixer == 2:
            x = _rwkv_layer(x, gain, mod3, rwkv_mu[r], rwkv_w_in[r], rwkv_w0[r], rwkv_w1[r], rwkv_w2[r],
                            rwkv_a0[r], rwkv_a1[r], rwkv_a2[r], rwkv_k_k[r], rwkv_k_a[r],
                            rwkv_r_k[r].reshape(-1), rwkv_ln_w[r], rwkv_ln_b[r], rwkv_w_out[r])
        else:
            x = _gla_layer(x, gain, mod3, gla_w_in[r], gla_alpha_w2[r], gla_alpha_b[r],
                           gla_norm_gain[r], gla_w_out[r])
    return x
```

```python
import functools

import jax
import jax.numpy as jnp
from jax import lax
from jax.experimental import pallas as pl
from jax.experimental.pallas import tpu as pltpu

F32 = jnp.float32
BF16 = jnp.bfloat16
HIGHEST = lax.Precision.HIGHEST

LANES = 128
SUBLANES = 8
VMEM_LIMIT_BYTES = 56 * 1024 * 1024

RMS_EPS = 1e-6
ROPE_THETA = 10000.0

DSA_HEADS = 16
DSA_KV_HEADS = 4
DSA_HEAD_DIM = 64
DSA_IDX_HEADS = 8
DSA_IDX_DIM = 128
DSA_TOPK = 256
DSA_QBLOCK = 128
DSA_KEY_CHUNK = 256
LOG2E = 1.4426950408889634
DSA_IDX_SCALE = (DSA_IDX_HEADS * DSA_IDX_DIM) ** -0.5

LRU_BLOCKS = 16
LRU_CONV = 4
LRU_C = 8.0
LRU_GROUP = 256

RWKV_HEAD_DIM = 64
RWKV_GN_EPS = 64e-5
RWKV_CHUNK = 64

GLA_HEADS = 4
GLA_GATE_RANK = 16
GLA_GATE_NORM = 16.0
GLA_CHUNK = 64

NEG_BIG = -1e30
NT_DIMS = (((1,), (1,)), ((), ()))


def _params(*semantics):
    return pltpu.CompilerParams(dimension_semantics=semantics,
                                vmem_limit_bytes=VMEM_LIMIT_BYTES)


def _resident(shape):
    return pl.BlockSpec(shape, lambda *_: (0,) * len(shape), pipeline_mode=pl.Buffered(1))


def _dot(a, b):
    return jnp.dot(a.astype(BF16), b.astype(BF16), preferred_element_type=F32)


def _dot_nt(a, b):
    return lax.dot_general(a.astype(BF16), b.astype(BF16), NT_DIMS,
                           preferred_element_type=F32)


def _split3(x):
    hi = x.astype(BF16)
    r1 = x - hi.astype(F32)
    mid = r1.astype(BF16)
    lo = (r1 - mid.astype(F32)).astype(BF16)
    return hi, mid, lo


def _dot_exact_lhs(m01, x):
    hi, mid, lo = _split3(x)
    return (jnp.dot(m01, hi, preferred_element_type=F32)
            + jnp.dot(m01, mid, preferred_element_type=F32)
            + jnp.dot(m01, lo, preferred_element_type=F32))


def _dot_exact_rhs(x, m01):
    hi, mid, lo = _split3(x)
    return (jnp.dot(hi, m01, preferred_element_type=F32)
            + jnp.dot(mid, m01, preferred_element_type=F32)
            + jnp.dot(lo, m01, preferred_element_type=F32))


def _group_sum(z, g01):
    cols = [_dot_exact_rhs(z[:, c * LANES:(c + 1) * LANES], g01)
            for c in range(z.shape[1] // LANES)]
    return cols[0] if len(cols) == 1 else jnp.concatenate(cols, axis=1)


def _group_sum_lanes(z, lo):
    cols = []
    for c in range(z.shape[1] // LANES):
        zb = z[:, c * LANES:(c + 1) * LANES]
        s_lo = jnp.sum(jnp.where(lo, zb, 0.0), axis=1, keepdims=True)
        s_hi = jnp.sum(jnp.where(lo, 0.0, zb), axis=1, keepdims=True)
        cols.append(jnp.where(lo, s_lo, s_hi))
    return cols[0] if len(cols) == 1 else jnp.concatenate(cols, axis=1)


def _silu(x):
    return x * jax.nn.sigmoid(x)


def _softplus(z):
    return jnp.maximum(z, 0.0) + jnp.log1p(jnp.exp(-jnp.abs(z)))


def _prenorm(x, gain, scale, shift):
    ms = jnp.mean(x * x, axis=-1, keepdims=True)
    y = x * lax.rsqrt(ms + RMS_EPS) * gain
    return y * (1.0 + scale) + shift


def _gated_residual(y, g, x, gate, w_out):
    a = (y * _silu(g)).astype(BF16)
    return x + gate * jnp.dot(a, w_out, preferred_element_type=F32)


def _mod_kernel(c_ref, w_ref, b_ref, o_ref):
    o_ref[0] = jnp.dot(_silu(c_ref[...]), w_ref[0], precision=HIGHEST,
                       preferred_element_type=F32) + b_ref[0]


def _modulation(c, mod_w, mod_b):
    depth, d, _ = mod_w.shape
    bsz = c.shape[0]
    return pl.pallas_call(
        _mod_kernel,
        out_shape=jax.ShapeDtypeStruct((depth, bsz, 3 * d), F32),
        grid=(depth, 3),
        in_specs=[pl.BlockSpec((bsz, d), lambda l, j: (0, 0)),
                  pl.BlockSpec((1, d, d), lambda l, j: (l, 0, j)),
                  pl.BlockSpec((1, 1, d), lambda l, j: (l, 0, j))],
        out_specs=pl.BlockSpec((1, bsz, d), lambda l, j: (l, 0, j)),
        compiler_params=_params("arbitrary", "arbitrary"),
        name="adaln_mod",
    )(c, mod_w, mod_b.reshape(depth, 1, 3 * d))


def _mod_specs(d):
    return [pl.BlockSpec((1, 1, d), lambda b, t, j=j: (b, 0, j)) for j in range(3)]


def _head_group_ones():
    r = jnp.arange(LANES) // DSA_HEAD_DIM
    return (r[:, None] == r[None, :]).astype(BF16)


def _lower_tri_ones(n):
    return (jnp.arange(n)[:, None] >= jnp.arange(n)[None, :]).astype(BF16)


def _rope_tables(seq, dim, reps):
    half = dim // 2
    inv_freq = ROPE_THETA ** (-jnp.arange(half, dtype=F32) / half)
    ang = jnp.arange(seq, dtype=F32)[:, None] * inv_freq[None, :]
    cos = jnp.concatenate([jnp.cos(ang), jnp.cos(ang)], axis=1)
    sin = jnp.concatenate([-jnp.sin(ang), jnp.sin(ang)], axis=1)
    return jnp.tile(cos, (1, reps)), jnp.tile(sin, (1, reps))


def _rope64(x, cos, sin, lane_lo):
    cols = []
    for c in range(x.shape[1] // LANES):
        xb = x[:, c * LANES:(c + 1) * LANES]
        rot = jnp.where(lane_lo, pltpu.roll(xb, 96, 1), pltpu.roll(xb, 32, 1))
        cols.append(xb * cos + rot * sin)
    return cols[0] if len(cols) == 1 else jnp.concatenate(cols, axis=1)


def _rope128(x, cos, sin):
    cols = []
    for c in range(x.shape[1] // LANES):
        xb = x[:, c * LANES:(c + 1) * LANES]
        cols.append(xb * cos + pltpu.roll(xb, 64, 1) * sin)
    return cols[0] if len(cols) == 1 else jnp.concatenate(cols, axis=1)


_DSA_Q = DSA_HEADS * DSA_HEAD_DIM
_DSA_KV2 = DSA_KV_HEADS * LANES
_DSA_QI = DSA_IDX_HEADS * DSA_IDX_DIM
_DSA_OFF_Q = 0
_DSA_OFF_G = _DSA_OFF_Q + _DSA_Q
_DSA_OFF_QI = _DSA_OFF_G + _DSA_Q
_DSA_OFF_K = _DSA_OFF_QI + _DSA_QI
_DSA_OFF_V = _DSA_OFF_K + _DSA_KV2
_DSA_OFF_KI = _DSA_OFF_V + _DSA_KV2
_DSA_OFF_WI = _DSA_OFF_KI + DSA_IDX_DIM
_DSA_COLS = _DSA_OFF_WI + LANES


def _dsa_proj_kernel(x_ref, gain_ref, shift_ref, scale_ref, w_ref, qgain_ref, kgain_ref,
                     cos64_ref, sin64_ref, cos128_ref, sin128_ref, g01_ref,
                     q_ref, g_ref, qi_ref, k_ref, v_ref, ki_ref, wi_ref):
    h = _prenorm(x_ref[0], gain_ref[...], scale_ref[0], shift_ref[0]).astype(BF16)
    g01 = g01_ref[...]
    cos64, sin64 = cos64_ref[...], sin64_ref[...]
    cos128, sin128 = cos128_ref[...], sin128_ref[...]
    lane = lax.broadcasted_iota(jnp.int32, cos64.shape, 1)
    lane_lo = (lane % DSA_HEAD_DIM) < DSA_HEAD_DIM // 2

    def head_norm_rope(raw, gain):
        ms = _group_sum(raw * raw, g01) * (1.0 / DSA_HEAD_DIM)
        return _rope64(raw * lax.rsqrt(ms + RMS_EPS) * gain, cos64, sin64, lane_lo)

    def col(off, n):
        return jnp.dot(h, w_ref[:, off:off + n], preferred_element_type=F32)

    tm = h.shape[0]
    row_lo64 = lax.broadcasted_iota(jnp.int32, (LANES, tm), 0) < DSA_HEAD_DIM
    q = head_norm_rope(col(_DSA_OFF_Q, _DSA_Q), qgain_ref[...]) * (LOG2E * DSA_HEAD_DIM ** -0.5)
    qt = q.T
    for hh in range(DSA_HEADS):
        blk = qt[(hh // 2) * LANES:(hh // 2 + 1) * LANES, :]
        keep = row_lo64 if hh % 2 == 0 else ~row_lo64
        q_ref[0, hh] = jnp.where(keep, blk, 0.0).astype(BF16)
    g_ref[0] = col(_DSA_OFF_G, _DSA_Q).astype(BF16)
    qit = _rope128(col(_DSA_OFF_QI, _DSA_QI), cos128, sin128).T.astype(BF16)
    for hh in range(DSA_IDX_HEADS):
        qi_ref[0, hh] = qit[hh * DSA_IDX_DIM:(hh + 1) * DSA_IDX_DIM, :]
    k2 = head_norm_rope(col(_DSA_OFF_K, _DSA_KV2), kgain_ref[...]).astype(BF16)
    for kv in range(DSA_KV_HEADS):
        k_ref[0, kv] = k2[:, kv * LANES:(kv + 1) * LANES]
    vt = col(_DSA_OFF_V, _DSA_KV2).T
    for kv in range(DSA_KV_HEADS):
        v_ref[0, kv] = jnp.where(row_lo64, vt[kv * LANES:(kv + 1) * LANES, :], 1.0).astype(BF16)
    ki_ref[0] = _rope128(col(_DSA_OFF_KI, DSA_IDX_DIM), cos128, sin128).astype(BF16)
    wit = (col(_DSA_OFF_WI, LANES) * DSA_IDX_SCALE).T
    wi_ref[0] = wit[0:DSA_IDX_HEADS, :]


def _dsa_weights(w_in):
    d = w_in.shape[0]
    q_end = _DSA_Q
    k_end = q_end + DSA_KV_HEADS * DSA_HEAD_DIM
    v_end = k_end + DSA_KV_HEADS * DSA_HEAD_DIM
    g_end = v_end + _DSA_Q
    qi_end = g_end + _DSA_QI
    wi_end = qi_end + DSA_IDX_HEADS
    wq, wk, wv, wg = w_in[:, :q_end], w_in[:, q_end:k_end], w_in[:, k_end:v_end], w_in[:, v_end:g_end]
    wqi, wwi, wki = w_in[:, g_end:qi_end], w_in[:, qi_end:wi_end], w_in[:, wi_end:]

    def dup(w):
        w = w.reshape(d, DSA_KV_HEADS, 1, DSA_HEAD_DIM)
        return jnp.broadcast_to(w, (d, DSA_KV_HEADS, 2, DSA_HEAD_DIM)).reshape(d, _DSA_KV2)

    wwi = jnp.pad(wwi, ((0, 0), (0, LANES - DSA_IDX_HEADS)))
    return jnp.concatenate([wq, wg, wqi, dup(wk), dup(wv), wki, wwi], axis=1).astype(BF16)


def _dsa_project(x, gain, mod3, w_in, q_gain, k_gain, tm=512):
    bsz, seq, d = x.shape
    tm = min(tm, seq)
    w = _dsa_weights(w_in)
    cos64, sin64 = _rope_tables(seq, DSA_HEAD_DIM, LANES // DSA_HEAD_DIM)
    cos128, sin128 = _rope_tables(seq, DSA_IDX_DIM, 1)
    shift_spec, scale_spec, _ = _mod_specs(d)
    table = pl.BlockSpec((tm, LANES), lambda b, t: (t, 0))
    row = lambda n, dt: jax.ShapeDtypeStruct((bsz, seq, n), dt)
    heads = lambda n: jax.ShapeDtypeStruct((bsz, n, seq, LANES), BF16)
    head_spec = lambda n: pl.BlockSpec((1, n, tm, LANES), lambda b, t: (b, 0, t, 0))
    heads_t = lambda n: jax.ShapeDtypeStruct((bsz, n, LANES, seq), BF16)
    head_t_spec = lambda n: pl.BlockSpec((1, n, LANES, tm), lambda b, t: (b, 0, 0, t))
    return pl.pallas_call(
        _dsa_proj_kernel,
        out_shape=[heads_t(DSA_HEADS), row(_DSA_Q, BF16), heads_t(DSA_IDX_HEADS),
                   heads(DSA_KV_HEADS), heads_t(DSA_KV_HEADS), row(DSA_IDX_DIM, BF16),
                   jax.ShapeDtypeStruct((bsz, DSA_IDX_HEADS, seq), F32)],
        grid=(bsz, seq // tm),
        in_specs=[pl.BlockSpec((1, tm, d), lambda b, t: (b, t, 0)),
                  _resident((1, d)), shift_spec, scale_spec, _resident(w.shape),
                  _resident((1, _DSA_Q)), _resident((1, _DSA_KV2)),
                  table, table, table, table, _resident((LANES, LANES))],
        out_specs=[head_t_spec(DSA_HEADS),
                   pl.BlockSpec((1, tm, _DSA_Q), lambda b, t: (b, t, 0)),
                   head_t_spec(DSA_IDX_HEADS),
                   head_spec(DSA_KV_HEADS),
                   head_t_spec(DSA_KV_HEADS),
                   pl.BlockSpec((1, tm, DSA_IDX_DIM), lambda b, t: (b, t, 0)),
                   pl.BlockSpec((1, DSA_IDX_HEADS, tm), lambda b, t: (b, 0, t))],
        compiler_params=_params("parallel", "parallel"),
        name="dsa_proj",
    )(x, gain.reshape(1, d), mod3, mod3, w,
      jnp.tile(q_gain, DSA_HEADS).reshape(1, _DSA_Q),
      jnp.tile(k_gain, _DSA_KV2 // DSA_HEAD_DIM).reshape(1, _DSA_KV2),
      cos64, sin64, cos128, sin128, _head_group_ones())


def _reduce_rows(x, op):
    part = op(x.reshape(x.shape[0] // 64, 64, x.shape[1]), axis=0)
    return op(part, axis=0, keepdims=True)


def _sortable_to_float(key):
    bits = jnp.where(key >= 0, key, key ^ jnp.int32(0x7FFFFFFF))
    return lax.bitcast_convert_type(bits, F32)


def _dsa_attn_kernel(n_sel, chunk, qi_ref, wi_ref, ki_ref, q_ref, k_ref, v_ref, tri_ref,
                     g_ref, x_ref, gate_ref, wout_ref, o_ref, score_ref, bias_ref, qk_ref):
    seq, qb = score_ref.shape
    per = chunk // qb
    needed = lax.div(pl.program_id(1) + per, per)
    for j in range(seq // chunk):
        width = chunk * (j + 1)
        block = _dsa_attn_block_all if width <= n_sel else _dsa_attn_block
        pl.when(needed == j + 1)(functools.partial(
            block, n_sel, width, qi_ref, wi_ref, ki_ref, q_ref, k_ref, v_ref,
            tri_ref, g_ref, x_ref, gate_ref, wout_ref, o_ref, score_ref, bias_ref, qk_ref))


def _dsa_attn_block(n_sel, width, qi_ref, wi_ref, ki_ref, q_ref, k_ref, v_ref, tri_ref,
                    g_ref, x_ref, gate_ref, wout_ref, o_ref, score_ref, bias_ref, qk_ref):
    blk = pl.program_id(1)
    qb = score_ref.shape[1]
    ki = ki_ref[0, 0:width, :]
    wi = wi_ref[0]
    stacked = DSA_HEADS // DSA_KV_HEADS

    def heads_of(ref, first):
        return jnp.concatenate([ref[0, first + j] for j in range(stacked)], axis=1)

    acc = None
    for grp in range(DSA_IDX_HEADS // stacked):
        logits = jnp.dot(ki, heads_of(qi_ref, grp * stacked), preferred_element_type=F32)
        for j in range(stacked):
            hh = grp * stacked + j
            term = wi[hh:hh + 1, :] * jnp.maximum(logits[:, j * qb:(j + 1) * qb], 0.0)
            acc = term if acc is None else acc + term
    q_pos = blk * qb + lax.broadcasted_iota(jnp.int32, (1, qb), 1)
    key_pos = lax.broadcasted_iota(jnp.int32, (width, 1), 0)
    score_ref[0:width, :] = jnp.where(key_pos <= q_pos, acc, -jnp.inf)

    k_sel = jnp.float32(n_sel)

    def count_ge(thr):
        return _reduce_rows(jnp.where(score_ref[0:width, :] >= thr, 1.0, 0.0), jnp.sum)

    def reaches(cand):
        return jnp.where(count_ge(_sortable_to_float(cand)) >= k_sel, 1, 0)

    def q_heads(kv):
        return jnp.concatenate([q_ref[0, kv * stacked + j] for j in range(stacked)], axis=1)

    half = width // 2
    per_step = 4

    def search_step(i, key):
        kv = i // 2
        rows = pl.ds(pl.multiple_of((i % 2) * half, half), half)
        qk_ref[kv, rows, :] = jnp.dot(k_ref[0, kv, rows, :], q_heads(kv),
                                      preferred_element_type=F32)
        for j in range(per_step):
            n = i * per_step + j
            cand = jnp.where(n == 0, jnp.int32(0),
                             key + lax.shift_left(jnp.int32(1), jnp.int32(31) - n))
            key = jnp.where(reaches(cand) > 0, cand, key)
        return key

    key0 = jnp.full((1, qb), -2 ** 31, jnp.int32)
    thr = _sortable_to_float(lax.fori_loop(0, 2 * DSA_KV_HEADS, search_step, key0))

    score = score_ref[0:width, :]
    gt = score > thr
    need = k_sel - _reduce_rows(jnp.where(gt, 1.0, 0.0), jnp.sum)
    take_all = q_pos < n_sel
    tri = tri_ref[...]
    run = jnp.zeros((1, qb), F32)
    for c in range(width // LANES):
        sl = slice(c * LANES, (c + 1) * LANES)
        eq = jnp.where(score[sl, :] == thr, 1.0, 0.0)
        incl = jnp.dot(tri, eq.astype(BF16), preferred_element_type=F32)
        tie_ok = (incl - eq + run) < need
        run = run + incl[LANES - 1:LANES, :]
        sel = gt[sl, :] | ((eq > 0.0) & tie_ok) | take_all
        causal = key_pos[sl, :] <= q_pos
        bias_ref[sl, :] = jnp.where(sel & causal, 0.0, NEG_BIG)

    _dsa_attn_tail(width, v_ref, g_ref, x_ref, gate_ref, wout_ref, o_ref, bias_ref, qk_ref)


def _dsa_attn_block_all(n_sel, width, qi_ref, wi_ref, ki_ref, q_ref, k_ref, v_ref, tri_ref,
                        g_ref, x_ref, gate_ref, wout_ref, o_ref, score_ref, bias_ref, qk_ref):
    qb = bias_ref.shape[1]
    stacked = DSA_HEADS // DSA_KV_HEADS
    q_pos = pl.program_id(1) * qb + lax.broadcasted_iota(jnp.int32, (1, qb), 1)
    key_pos = lax.broadcasted_iota(jnp.int32, (width, 1), 0)
    bias_ref[0:width, :] = jnp.where(key_pos <= q_pos, 0.0, NEG_BIG)
    for kv in range(DSA_KV_HEADS):
        q4 = jnp.concatenate([q_ref[0, kv * stacked + j] for j in range(stacked)], axis=1)
        qk_ref[kv, 0:width, :] = jnp.dot(k_ref[0, kv, 0:width, :], q4, preferred_element_type=F32)
    _dsa_attn_tail(width, v_ref, g_ref, x_ref, gate_ref, wout_ref, o_ref, bias_ref, qk_ref)


def _dsa_attn_tail(width, v_ref, g_ref, x_ref, gate_ref, wout_ref, o_ref, bias_ref, qk_ref):
    qb = bias_ref.shape[1]
    stacked = DSA_HEADS // DSA_KV_HEADS
    bias = bias_ref[0:width, :]
    outs = []
    for kv in range(DSA_KV_HEADS):
        s4 = qk_ref[kv, 0:width, :]
        ps = []
        for j in range(stacked):
            s = s4[:, j * qb:(j + 1) * qb] + bias
            ps.append(jnp.exp2(s - _reduce_rows(s, jnp.max)).astype(BF16))
        ov = jnp.dot(v_ref[0, kv, :, 0:width], jnp.concatenate(ps, axis=1),
                     preferred_element_type=F32)
        ov = ov[0:DSA_HEAD_DIM, :] / ov[DSA_HEAD_DIM:DSA_HEAD_DIM + 1, :]
        for c in range(stacked // 2):
            even = ov[:, (2 * c) * qb:(2 * c + 1) * qb]
            odd = ov[:, (2 * c + 1) * qb:(2 * c + 2) * qb]
            outs.append(jnp.concatenate([even, odd], axis=0).T)
    o_ref[0] = _gated_residual(jnp.concatenate(outs, axis=1), g_ref[0].astype(F32), x_ref[0],
                               gate_ref[0], wout_ref[...])


def _dsa_attention(q, qi, k2, vt, ki, wi, g, x, mod3, w_out):
    bsz, seq, d = x.shape
    qb = DSA_QBLOCK
    n_sel = min(DSA_TOPK, seq // 4)
    chunk = min(DSA_KEY_CHUNK, seq // 2)
    return pl.pallas_call(
        functools.partial(_dsa_attn_kernel, n_sel, chunk),
        out_shape=jax.ShapeDtypeStruct((bsz, seq, d), F32),
        grid=(bsz, seq // qb),
        in_specs=[pl.BlockSpec((1, DSA_IDX_HEADS, LANES, qb), lambda b, i: (b, 0, 0, i)),
                  pl.BlockSpec((1, DSA_IDX_HEADS, qb), lambda b, i: (b, 0, i)),
                  pl.BlockSpec((1, seq, DSA_IDX_DIM), lambda b, i: (b, 0, 0)),
                  pl.BlockSpec((1, DSA_HEADS, LANES, qb), lambda b, i: (b, 0, 0, i)),
                  pl.BlockSpec((1, DSA_KV_HEADS, seq, LANES), lambda b, i: (b, 0, 0, 0)),
                  pl.BlockSpec((1, DSA_KV_HEADS, LANES, seq), lambda b, i: (b, 0, 0, 0)),
                  _resident((LANES, LANES)),
                  pl.BlockSpec((1, qb, _DSA_Q), lambda b, i: (b, i, 0)),
                  pl.BlockSpec((1, qb, d), lambda b, i: (b, i, 0)),
                  _mod_specs(d)[2],
                  _resident((_DSA_Q, d))],
        out_specs=pl.BlockSpec((1, qb, d), lambda b, i: (b, i, 0)),
        scratch_shapes=[pltpu.VMEM((seq, qb), F32), pltpu.VMEM((seq, qb), F32),
                        pltpu.VMEM((DSA_KV_HEADS, seq, (DSA_HEADS // DSA_KV_HEADS) * qb), F32)],
        compiler_params=_params("parallel", "parallel"),
        name="dsa_attn",
    )(qi, wi, ki, q, k2, vt, _lower_tri_ones(LANES), g, x, mod3, w_out.astype(BF16))


def _dsa_layer(x, gain, mod3, w_in, q_gain, k_gain, w_out):
    q, g, qi, k2, vt, ki, wi = _dsa_project(x, gain, mod3, w_in, q_gain, k_gain)
    return _dsa_attention(q, qi, k2, vt, ki, wi, g, x, mod3, w_out)


def _lru_kernel(x_ref, gain_ref, shift_ref, scale_ref, gate_ref, win_ref, cw_ref, cb_ref,
                wa_ref, ba_ref, wx_ref, bx_ref, lam_ref, wout_ref, o_ref,
                h_buf, ubuf, a_buf, b_buf, hs_buf, g_buf, h_carry):
    tt = x_ref.shape[1]
    width = ubuf.shape[1]
    halo = SUBLANES

    @pl.when(pl.program_id(1) == 0)
    def _():
        ubuf[0:halo, :] = jnp.zeros((halo, width), F32)
        h_carry[...] = jnp.zeros_like(h_carry)

    h_buf[...] = _prenorm(x_ref[0], gain_ref[...], scale_ref[0], shift_ref[0]).astype(BF16)
    ubuf[halo:halo + tt, :] = jnp.dot(h_buf[...], win_ref[:, 0:width], preferred_element_type=F32)
    cw = cw_ref[...]
    u = cb_ref[...]
    for j in range(LRU_CONV):
        start = halo - (LRU_CONV - 1) + j
        u = u + cw[j:j + 1, :] * ubuf[start:start + tt, :]
    ubuf[0:halo, :] = ubuf[tt:tt + halo, :]

    g_buf[...] = jnp.dot(h_buf[...], win_ref[:, width:2 * width], preferred_element_type=F32)

    sp = _softplus(-lam_ref[...])
    for c in range(width // LRU_GROUP):
        sl = slice(c * LRU_GROUP, (c + 1) * LRU_GROUP)
        uc = u[:, sl]
        ub = uc.astype(BF16)
        r = jax.nn.sigmoid(jnp.dot(ub, wa_ref[c], preferred_element_type=F32) + ba_ref[:, sl])
        i = jax.nn.sigmoid(jnp.dot(ub, wx_ref[c], preferred_element_type=F32) + bx_ref[:, sl])
        a = jnp.exp(-LRU_C * r * sp[:, sl])
        a_buf[:, sl] = a
        b_buf[:, sl] = jnp.sqrt(1.0 - a * a) * (i * uc)

    row = lax.broadcasted_iota(jnp.int32, (SUBLANES, width), 0)

    def group(gi, h_prev):
        r0 = pl.multiple_of(gi * SUBLANES, SUBLANES)
        a = a_buf[pl.ds(r0, SUBLANES), :]
        b = b_buf[pl.ds(r0, SUBLANES), :]
        for s in (1, 2, 4):
            ok = row >= s
            b = jnp.where(ok, a * pltpu.roll(b, s, 0) + b, b)
            a = jnp.where(ok, a * pltpu.roll(a, s, 0), a)
        h = a * h_prev + b
        hs_buf[pl.ds(r0, SUBLANES), :] = h
        return jnp.broadcast_to(h[SUBLANES - 1:SUBLANES, :], (SUBLANES, width))

    h_carry[...] = lax.fori_loop(0, tt // SUBLANES, group, h_carry[...])
    o_ref[0] = _gated_residual(hs_buf[...], g_buf[...], x_ref[0], gate_ref[0], wout_ref[...])


def _lru_gate_blocks(w):
    nb, bd, _ = w.shape
    per = LRU_GROUP // bd
    w = w.reshape(nb // per, per, bd, bd)
    eye = jnp.eye(per, dtype=w.dtype)
    return jnp.einsum('gpcd,pq->gpcqd', w, eye).reshape(nb // per, LRU_GROUP, LRU_GROUP).astype(BF16)


def _lru_layer(x, gain, mod3, w_in, conv_w, conv_b, gate_a_w, gate_a_b, gate_x_w, gate_x_b, lam,
               w_out, tt=512):
    bsz, seq, d = x.shape
    tt = min(tt, seq)
    width = w_in.shape[1] // 2
    ngroups = width // LRU_GROUP
    tile = pl.BlockSpec((1, tt, d), lambda b, t: (b, t, 0))
    vec = lambda: _resident((1, width))
    gatew = lambda: _resident((ngroups, LRU_GROUP, LRU_GROUP))
    cw = jnp.pad(conv_w, ((0, SUBLANES - LRU_CONV), (0, 0)))
    return pl.pallas_call(
        _lru_kernel,
        out_shape=jax.ShapeDtypeStruct((bsz, seq, d), F32),
        grid=(bsz, seq // tt),
        in_specs=[tile, _resident((1, d)), *_mod_specs(d), _resident((d, 2 * width)),
                  _resident((SUBLANES, width)), vec(), gatew(), vec(), gatew(), vec(), vec(),
                  _resident((width, d))],
        out_specs=tile,
        scratch_shapes=[pltpu.VMEM((tt, d), BF16),
                        pltpu.VMEM((tt + SUBLANES, width), F32),
                        pltpu.VMEM((tt, width), F32), pltpu.VMEM((tt, width), F32),
                        pltpu.VMEM((tt, width), F32), pltpu.VMEM((tt, width), F32),
                        pltpu.VMEM((SUBLANES, width), F32)],
        compiler_params=_params("parallel", "arbitrary"),
        name="rglru_layer",
    )(x, gain.reshape(1, d), mod3, mod3, mod3, w_in.astype(BF16), cw, conv_b.reshape(1, width),
      _lru_gate_blocks(gate_a_w), gate_a_b.reshape(1, width),
      _lru_gate_blocks(gate_x_w), gate_x_b.reshape(1, width), lam.reshape(1, width),
      w_out.astype(BF16))


def _rwkv_kernel(x_ref, gain_ref, shift_ref, scale_ref, gate_ref, mu_ref, w_ref, w1_ref, w2_ref,
                 a1_ref, a2_ref, w0_ref, a0_ref, kkw_ref, ka_ref, rk_ref, lnw_ref, lnb_ref,
                 ltri_ref, wout_ref, o_ref,
                 r_s, lw_s, k_s, v_s, kk_s, a_s, bonus_s, g_s, y_s, carry, state):
    tm = x_ref.shape[1]
    d = x_ref.shape[2]
    cs = RWKV_CHUNK
    hd = RWKV_HEAD_DIM

    @pl.when(pl.program_id(1) == 0)
    def _():
        carry[...] = jnp.zeros_like(carry)
        state[...] = jnp.zeros_like(state)

    lo_t = lax.broadcasted_iota(jnp.int32, (tm, LANES), 1) < hd
    h = _prenorm(x_ref[0], gain_ref[...], scale_ref[0], shift_ref[0])
    first = lax.broadcasted_iota(jnp.int32, (tm, d), 0) == 0
    h_prev = jnp.where(first, carry[0:1, :], pltpu.roll(h, 1, 0))
    carry[...] = jnp.broadcast_to(h[tm - 1:tm, :], carry.shape)
    delta = h_prev - h
    mu = mu_ref[...]

    def mix(n):
        return (h + delta * mu[n:n + 1, :]).astype(BF16)

    r = jnp.dot(mix(0), w_ref[0], preferred_element_type=F32)
    k = jnp.dot(mix(1), w_ref[1], preferred_element_type=F32)
    v = jnp.dot(mix(2), w_ref[2], preferred_element_type=F32)
    g_s[...] = jnp.dot(mix(3), w_ref[3], preferred_element_type=F32)
    w_lora = _dot(jnp.tanh(jnp.dot(mix(4), w1_ref[...], preferred_element_type=F32)), w2_ref[...])
    w_log = -_softplus(-(w0_ref[...] + w_lora)) - 0.5
    lw_s[...] = -jnp.exp(w_log)
    a_lora = _dot(jnp.dot(mix(5), a1_ref[...], preferred_element_type=F32), a2_ref[...])
    a = jax.nn.sigmoid(a0_ref[...] + a_lora)
    kk = k * kkw_ref[...]
    kk = kk / jnp.maximum(jnp.sqrt(_group_sum_lanes(kk * kk, lo_t)), 1e-12)
    k = k * (1.0 + (a - 1.0) * ka_ref[...])
    r_s[...] = r
    k_s[...] = k
    v_s[...] = v
    kk_s[...] = kk
    a_s[...] = a
    bonus_s[...] = _group_sum_lanes(r * k * rk_ref[...], lo_t) * v

    lane = lax.broadcasted_iota(jnp.int32, (cs, LANES), 1)
    lo = lane < hd
    ri = lax.broadcasted_iota(jnp.int32, (LANES, LANES), 0)
    ci = lax.broadcasted_iota(jnp.int32, (LANES, LANES), 1)
    same = (ri < cs) == (ci < cs)
    strict = same & (ci < ri)
    incl = same & (ci <= ri)
    eye = jnp.where(ri == ci, 1.0, 0.0)
    ltri = ltri_ref[...]

    def stack(z):
        return jnp.concatenate([z, z], axis=0)

    def split_heads(z):
        return jnp.concatenate([jnp.where(lo, z, 0.0), jnp.where(lo, 0.0, z)], axis=0)

    def own(z):
        return jnp.where(lo, z[0:cs], z[cs:2 * cs])

    pairs = range(d // LANES)
    lanes = [slice(p * LANES, (p + 1) * LANES) for p in pairs]

    def chunk(c, _):
        rows = pl.ds(pl.multiple_of(c * cs, cs), cs)
        lw = lw_s[rows, :]
        g_inc = _dot_exact_lhs(ltri, lw)
        g_last = g_inc[cs - 1:cs, :]
        kk = kk_s[rows, :]
        kc = k_s[rows, :]
        vc = v_s[rows, :]
        bvec = kk * a_s[rows, :]
        e_neg = jnp.exp(-g_inc)
        e_end = jnp.exp(g_last - g_inc)
        at = -kk * jnp.exp(g_inc - lw)
        rt = r_s[rows, :] * jnp.exp(g_inc)
        bt = bvec * e_neg
        kt = kc * e_neg
        bend = (bvec * e_end).astype(BF16)
        kend = (kc * e_end).astype(BF16)
        decay = jnp.exp(g_last)

        lhs = [jnp.concatenate([at[:, s], rt[:, s]], axis=0).astype(BF16) for s in lanes]
        nbk = [_dot_nt(lhs[p], jnp.concatenate([split_heads(bt[:, lanes[p]]),
                                                split_heads(kt[:, lanes[p]])], axis=0))
               for p in pairs]
        n_ab = [jnp.where(strict, stack(nbk[p][0:cs, 0:LANES]), 0.0) for p in pairs]
        n_rb = [jnp.where(incl, stack(nbk[p][cs:2 * cs, 0:LANES]), 0.0).astype(BF16) for p in pairs]
        n_ak = [jnp.where(strict, stack(nbk[p][0:cs, LANES:2 * LANES]), 0.0) for p in pairs]
        n_rk = [jnp.where(incl, stack(nbk[p][cs:2 * cs, LANES:2 * LANES]), 0.0) for p in pairs]

        inv = [eye + n_ab[p] for p in pairs]
        pw = [_dot(n_ab[p], n_ab[p]) for p in pairs]
        for _stage in range(4):
            res = [_dot(pw[p], jnp.concatenate([pw[p], inv[p]], axis=1)) for p in pairs]
            pw = [res[p][:, 0:LANES] for p in pairs]
            inv = [inv[p] + res[p][:, LANES:2 * LANES] for p in pairs]
        inv = [(inv[p] + _dot(pw[p], inv[p])).astype(BF16) for p in pairs]

        vv = [stack(vc[:, s]).astype(BF16) for s in lanes]
        nv = [_dot(jnp.concatenate([n_ak[p], n_rk[p]], axis=0), vv[p]) for p in pairs]
        w1 = [own(nv[p][0:LANES]) for p in pairs]
        tz = [_dot(inv[p], jnp.concatenate([stack(at[:, lanes[p]]), stack(w1[p])], axis=1))
              for p in pairs]
        a2 = [own(tz[p][:, 0:LANES]) for p in pairs]
        u0 = [own(tz[p][:, LANES:2 * LANES]) for p in pairs]
        rz = [_dot(n_rb[p], jnp.concatenate([stack(a2[p]), stack(u0[p])], axis=1)) for p in pairs]
        r2 = [rt[:, lanes[p]] + own(rz[p][:, 0:LANES]) for p in pairs]
        y0 = [own(rz[p][:, LANES:2 * LANES]) + own(nv[p][LANES:2 * LANES]) for p in pairs]
        mlr = [jnp.where(same, _dot(a2[p].T, bend[:, lanes[p]]), 0.0) for p in pairs]
        c0 = [jnp.where(same, _dot(jnp.concatenate([u0[p], vc[:, lanes[p]]], axis=0).T,
                                   jnp.concatenate([bend[:, lanes[p]], kend[:, lanes[p]]], axis=0)), 0.0)
              for p in pairs]

        ys = []
        for p in pairs:
            s_bd = state[p]
            ys.append(_dot_nt(r2[p], s_bd) + y0[p])
            state[p] = s_bd * decay[:, lanes[p]] + _dot(s_bd, mlr[p]) + c0[p]
        y = jnp.concatenate(ys, axis=1)
        mean = _group_sum_lanes(y, lo) * (1.0 / hd)
        yc = y - mean
        var = _group_sum_lanes(yc * yc, lo) * (1.0 / hd)
        y_s[rows, :] = (yc * lax.rsqrt(var + RWKV_GN_EPS) * lnw_ref[...] + lnb_ref[...]
                        + bonus_s[rows, :])
        return 0

    lax.fori_loop(0, tm // cs, chunk, 0, unroll=2)
    o_ref[0] = _gated_residual(y_s[...], g_s[...], x_ref[0], gate_ref[0], wout_ref[...])


def _rwkv_layer(x, gain, mod3, mu, w_in, w0, w1, w2, a0, a1, a2, k_k, k_a, r_k, ln_w, ln_b, w_out,
                tm=512):
    bsz, seq, d = x.shape
    tm = min(tm, seq)
    rank = w1.shape[1]
    pad_c = lambda w: jnp.pad(w, ((0, 0), (0, LANES - rank))).astype(BF16)
    pad_r = lambda w: jnp.pad(w, ((0, LANES - rank), (0, 0))).astype(BF16)
    vec = lambda p: p.reshape(1, d)
    tile = pl.BlockSpec((1, tm, d), lambda b, t: (b, t, 0))
    buf = pltpu.VMEM((tm, d), F32)
    return pl.pallas_call(
        _rwkv_kernel,
        out_shape=jax.ShapeDtypeStruct((bsz, seq, d), F32),
        grid=(bsz, seq // tm),
        in_specs=[tile, _resident((1, d)), *_mod_specs(d), _resident((SUBLANES, d)),
                  _resident((4, d, d)), _resident((d, LANES)), _resident((LANES, d)),
                  _resident((d, LANES)), _resident((LANES, d))]
                 + [_resident((1, d))] * 7
                 + [_resident((RWKV_CHUNK, RWKV_CHUNK)), _resident((d, d))],
        out_specs=tile,
        scratch_shapes=[buf] * 9 + [pltpu.VMEM((SUBLANES, d), F32),
                                    pltpu.VMEM((d // LANES, LANES, LANES), F32)],
        compiler_params=_params("parallel", "arbitrary"),
        name="rwkv_layer",
    )(x, gain.reshape(1, d), mod3, mod3, mod3,
      jnp.pad(mu, ((0, SUBLANES - mu.shape[0]), (0, 0))),
      w_in.astype(BF16), pad_c(w1), pad_r(w2), pad_c(a1), pad_r(a2), vec(w0), vec(a0),
      vec(k_k), vec(k_a), vec(r_k), vec(ln_w), vec(ln_b),
      _lower_tri_ones(RWKV_CHUNK), w_out.astype(BF16))


def _gla_kernel(x_ref, gain_ref, shift_ref, scale_ref, gate_ref, win_ref, w2_ref, ab_ref,
                ngain_ref, ltri_ref, wout_ref, o_ref, o_buf, state):
    tt = x_ref.shape[1]
    key_dim = w2_ref.shape[1]
    val_dim = wout_ref.shape[0]
    dk = key_dim // GLA_HEADS
    dv = val_dim // GLA_HEADS
    cs = GLA_CHUNK

    @pl.when(pl.program_id(1) == 0)
    def _():
        state[...] = jnp.zeros_like(state)

    h = _prenorm(x_ref[0], gain_ref[...], scale_ref[0], shift_ref[0]).astype(BF16)

    def col(off, n):
        return jnp.dot(h, win_ref[:, off:off + n], preferred_element_type=F32)

    q = col(0, key_dim)
    k = col(key_dim, key_dim)
    v = col(2 * key_dim, val_dim)
    g = col(2 * key_dim + val_dim, val_dim)
    a_low = col(2 * key_dim + 2 * val_dim, LANES)

    ri = lax.broadcasted_iota(jnp.int32, (cs, cs), 0)
    ci = lax.broadcasted_iota(jnp.int32, (cs, cs), 1)
    causal = ci <= ri
    ltri = ltri_ref[...]
    z = _dot(a_low, w2_ref[...]) + ab_ref[...]
    log_alpha = -_softplus(-z) * (1.0 / GLA_GATE_NORM)

    chunks = range(tt // cs)
    heads = range(GLA_HEADS)
    rows = [slice(c * cs, (c + 1) * cs) for c in chunks]
    ks = [slice(hh * dk, (hh + 1) * dk) for hh in heads]
    vs = [slice(hh * dv, (hh + 1) * dv) for hh in heads]
    q_dec, k_inv, k_end, decay = [], [], [], []
    for c in chunks:
        cum = _dot_exact_lhs(ltri, log_alpha[rows[c], :])
        last = cum[cs - 1:cs, :]
        kc = k[rows[c], :]
        q_dec.append((q[rows[c], :] * (dk ** -0.5) * jnp.exp(cum)).astype(BF16))
        k_inv.append(kc * jnp.exp(-cum))
        k_end.append(kc * jnp.exp(last - cum))
        decay.append(jnp.exp(last))
    att = [[jnp.where(causal, _dot_nt(q_dec[c][:, ks[hh]], k_inv[c][:, ks[hh]]), 0.0)
            for hh in heads] for c in chunks]
    o_intra = [[_dot(att[c][hh], v[rows[c], vs[hh]]) for hh in heads] for c in chunks]
    upd = [[_dot(v[rows[c], vs[hh]].T, k_end[c][:, ks[hh]]) for hh in heads] for c in chunks]
    start = [[None] * GLA_HEADS for _ in chunks]
    for hh in heads:
        st = state[hh]
        for c in chunks:
            start[c][hh] = st
            st = st * decay[c][:, ks[hh]] + upd[c][hh]
        state[hh] = st
    for c in chunks:
        for hh in heads:
            o = o_intra[c][hh] + _dot_nt(q_dec[c][:, ks[hh]], start[c][hh])
            ms = jnp.mean(o * o, axis=-1, keepdims=True)
            o_buf[rows[c], vs[hh]] = o * lax.rsqrt(ms + RMS_EPS) * ngain_ref[...]

    o_ref[0] = _gated_residual(o_buf[...], g, x_ref[0], gate_ref[0], wout_ref[...])


def _gla_layer(x, gain, mod3, w_in, alpha_w2, alpha_b, norm_gain, w_out, tt=512):
    bsz, seq, d = x.shape
    tt = min(tt, seq)
    key_dim, val_dim = alpha_w2.shape[1], w_out.shape[0]
    dk, dv = key_dim // GLA_HEADS, val_dim // GLA_HEADS
    w = jnp.pad(w_in, ((0, 0), (0, LANES - GLA_GATE_RANK))).astype(BF16)
    w2 = jnp.pad(alpha_w2, ((0, LANES - GLA_GATE_RANK), (0, 0))).astype(BF16)
    tile = pl.BlockSpec((1, tt, d), lambda b, t: (b, t, 0))
    return pl.pallas_call(
        _gla_kernel,
        out_shape=jax.ShapeDtypeStruct((bsz, seq, d), F32),
        grid=(bsz, seq // tt),
        in_specs=[tile, _resident((1, d)), *_mod_specs(d), _resident(w.shape),
                  _resident((LANES, key_dim)), _resident((1, key_dim)), _resident((1, dv)),
                  _resident((GLA_CHUNK, GLA_CHUNK)), _resident((val_dim, d))],
        out_specs=tile,
        scratch_shapes=[pltpu.VMEM((tt, val_dim), F32), pltpu.VMEM((GLA_HEADS, dv, dk), F32)],
        compiler_params=_params("parallel", "arbitrary"),
        name="gla_layer",
    )(x, gain.reshape(1, d), mod3, mod3, mod3, w, w2, alpha_b.reshape(1, key_dim),
      norm_gain.reshape(1, dv), _lower_tri_ones(GLA_CHUNK), w_out.astype(BF16))


def kernel(x, c, ln_gain, mod_w, mod_b, dsa_w_in, dsa_q_gain, dsa_k_gain, dsa_w_out, lru_w_in, lru_conv_w, lru_conv_b, lru_gate_a_w, lru_gate_a_b, lru_gate_x_w, lru_gate_x_b, lru_lambda, lru_w_out, rwkv_mu, rwkv_w_in, rwkv_w0, rwkv_w1, rwkv_w2, rwkv_a0, rwkv_a1, rwkv_a2, rwkv_k_k, rwkv_k_a, rwkv_r_k, rwkv_ln_w, rwkv_ln_b, rwkv_w_out, gla_w_in, gla_alpha_w2, gla_alpha_b, gla_norm_gain, gla_w_out):
    depth = mod_w.shape[0]
    bsz, _, d = x.shape
    mod = _modulation(c, mod_w, mod_b)
    for layer in range(depth):
        mixer, r = layer % 4, layer // 4
        mod3 = mod[layer].reshape(bsz, 1, 3 * d)
        gain = ln_gain[layer]
        if mixer == 0:
            x = _dsa_layer(x, gain, mod3, dsa_w_in[r], dsa_q_gain[r], dsa_k_gain[r], dsa_w_out[r])
        elif mixer == 1:
            x = _lru_layer(x, gain, mod3, lru_w_in[r], lru_conv_w[r], lru_conv_b[r], lru_gate_a_w[r],
                           lru_gate_a_b[r], lru_gate_x_w[r], lru_gate_x_b[r], lru_lambda[r], lru_w_out[r])
        elif mixer == 2:
            x = _rwkv_layer(x, gain, mod3, rwkv_mu[r], rwkv_w_in[r], rwkv_w0[r], rwkv_w1[r], rwkv_w2[r],
                            rwkv_a0[r], rwkv_a1[r], rwkv_a2[r], rwkv_k_k[r], rwkv_k_a[r],
                            rwkv_r_k[r].reshape(-1), rwkv_ln_w[r], rwkv_ln_b[r], rwkv_w_out[r])
        else:
            x = _gla_layer(x, gain, mod3, gla_w_in[r], gla_alpha_w2[r], gla_alpha_b[r],
                           gla_norm_gain[r], gla_w_out[r])
    return x
```

```python
import functools

import jax
import jax.numpy as jnp
from jax import lax
from jax.experimental import pallas as pl
from jax.experimental.pallas import tpu as pltpu

F32 = jnp.float32
BF16 = jnp.bfloat16
HIGHEST = lax.Precision.HIGHEST

LANES = 128
SUBLANES = 8
VMEM_LIMIT_BYTES = 56 * 1024 * 1024

RMS_EPS = 1e-6
ROPE_THETA = 10000.0

DSA_HEADS = 16
DSA_KV_HEADS = 4
DSA_HEAD_DIM = 64
DSA_IDX_HEADS = 8
DSA_IDX_DIM = 128
DSA_TOPK = 256
DSA_QBLOCK = 128
DSA_KEY_CHUNK = 256
LOG2E = 1.4426950408889634
DSA_IDX_SCALE = (DSA_IDX_HEADS * DSA_IDX_DIM) ** -0.5

LRU_BLOCKS = 16
LRU_CONV = 4
LRU_C = 8.0
LRU_GROUP = 256

RWKV_HEAD_DIM = 64
RWKV_GN_EPS = 64e-5
RWKV_CHUNK = 64

GLA_HEADS = 4
GLA_GATE_RANK = 16
GLA_GATE_NORM = 16.0
GLA_CHUNK = 64

NEG_BIG = -1e30
NT_DIMS = (((1,), (1,)), ((), ()))


def _params(*semantics):
    return pltpu.CompilerParams(dimension_semantics=semantics,
                                vmem_limit_bytes=VMEM_LIMIT_BYTES)


def _resident(shape):
    return pl.BlockSpec(shape, lambda *_: (0,) * len(shape), pipeline_mode=pl.Buffered(1))


def _dot(a, b):
    return jnp.dot(a.astype(BF16), b.astype(BF16), preferred_element_type=F32)


def _dot_nt(a, b):
    return lax.dot_general(a.astype(BF16), b.astype(BF16), NT_DIMS,
                           preferred_element_type=F32)


def _split3(x):
    hi = x.astype(BF16)
    r1 = x - hi.astype(F32)
    mid = r1.astype(BF16)
    lo = (r1 - mid.astype(F32)).astype(BF16)
    return hi, mid, lo


def _dot_exact_lhs(m01, x):
    hi, mid, lo = _split3(x)
    return (jnp.dot(m01, hi, preferred_element_type=F32)
            + jnp.dot(m01, mid, preferred_element_type=F32)
            + jnp.dot(m01, lo, preferred_element_type=F32))


def _dot_exact_rhs(x, m01):
    hi, mid, lo = _split3(x)
    return (jnp.dot(hi, m01, preferred_element_type=F32)
            + jnp.dot(mid, m01, preferred_element_type=F32)
            + jnp.dot(lo, m01, preferred_element_type=F32))


def _group_sum(z, g01):
    cols = [_dot_exact_rhs(z[:, c * LANES:(c + 1) * LANES], g01)
            for c in range(z.shape[1] // LANES)]
    return cols[0] if len(cols) == 1 else jnp.concatenate(cols, axis=1)


def _group_sum_lanes(z, lo):
    cols = []
    for c in range(z.shape[1] // LANES):
        zb = z[:, c * LANES:(c + 1) * LANES]
        s_lo = jnp.sum(jnp.where(lo, zb, 0.0), axis=1, keepdims=True)
        s_hi = jnp.sum(jnp.where(lo, 0.0, zb), axis=1, keepdims=True)
        cols.append(jnp.where(lo, s_lo, s_hi))
    return cols[0] if len(cols) == 1 else jnp.concatenate(cols, axis=1)


def _silu(x):
    return x * jax.nn.sigmoid(x)


def _softplus(z):
    return jnp.maximum(z, 0.0) + jnp.log(1.0 + jnp.exp(-jnp.abs(z)))


def _prenorm(x, gain, scale, shift):
    ms = jnp.mean(x * x, axis=-1, keepdims=True)
    y = x * lax.rsqrt(ms + RMS_EPS) * gain
    return y * (1.0 + scale) + shift


def _gated_residual(y, g, x, gate, w_out):
    a = (y * _silu(g)).astype(BF16)
    return x + gate * jnp.dot(a, w_out, preferred_element_type=F32)


def _mod_kernel(c_ref, w_ref, b_ref, o_ref):
    o_ref[0] = jnp.dot(_silu(c_ref[...]), w_ref[0], precision=HIGHEST,
                       preferred_element_type=F32) + b_ref[0]


def _modulation(c, mod_w, mod_b):
    depth, d, _ = mod_w.shape
    bsz = c.shape[0]
    return pl.pallas_call(
        _mod_kernel,
        out_shape=jax.ShapeDtypeStruct((depth, bsz, 3 * d), F32),
        grid=(depth, 3),
        in_specs=[pl.BlockSpec((bsz, d), lambda l, j: (0, 0)),
                  pl.BlockSpec((1, d, d), lambda l, j: (l, 0, j)),
                  pl.BlockSpec((1, 1, d), lambda l, j: (l, 0, j))],
        out_specs=pl.BlockSpec((1, bsz, d), lambda l, j: (l, 0, j)),
        compiler_params=_params("arbitrary", "arbitrary"),
        name="adaln_mod",
    )(c, mod_w, mod_b.reshape(depth, 1, 3 * d))


def _mod_specs(d):
    return [pl.BlockSpec((1, 1, d), lambda b, t, j=j: (b, 0, j)) for j in range(3)]


def _head_group_ones():
    r = jnp.arange(LANES) // DSA_HEAD_DIM
    return (r[:, None] == r[None, :]).astype(BF16)


def _lower_tri_ones(n):
    return (jnp.arange(n)[:, None] >= jnp.arange(n)[None, :]).astype(BF16)


def _rope_tables(seq, dim, reps):
    half = dim // 2
    inv_freq = ROPE_THETA ** (-jnp.arange(half, dtype=F32) / half)
    ang = jnp.arange(seq, dtype=F32)[:, None] * inv_freq[None, :]
    cos = jnp.concatenate([jnp.cos(ang), jnp.cos(ang)], axis=1)
    sin = jnp.concatenate([-jnp.sin(ang), jnp.sin(ang)], axis=1)
    return jnp.tile(cos, (1, reps)), jnp.tile(sin, (1, reps))


def _rope64(x, cos, sin, lane_lo):
    cols = []
    for c in range(x.shape[1] // LANES):
        xb = x[:, c * LANES:(c + 1) * LANES]
        rot = jnp.where(lane_lo, pltpu.roll(xb, 96, 1), pltpu.roll(xb, 32, 1))
        cols.append(xb * cos + rot * sin)
    return cols[0] if len(cols) == 1 else jnp.concatenate(cols, axis=1)


def _rope128(x, cos, sin):
    cols = []
    for c in range(x.shape[1] // LANES):
        xb = x[:, c * LANES:(c + 1) * LANES]
        cols.append(xb * cos + pltpu.roll(xb, 64, 1) * sin)
    return cols[0] if len(cols) == 1 else jnp.concatenate(cols, axis=1)


_DSA_Q = DSA_HEADS * DSA_HEAD_DIM
_DSA_KV2 = DSA_KV_HEADS * LANES
_DSA_QI = DSA_IDX_HEADS * DSA_IDX_DIM
_DSA_OFF_Q = 0
_DSA_OFF_G = _DSA_OFF_Q + _DSA_Q
_DSA_OFF_QI = _DSA_OFF_G + _DSA_Q
_DSA_OFF_K = _DSA_OFF_QI + _DSA_QI
_DSA_OFF_V = _DSA_OFF_K + _DSA_KV2
_DSA_OFF_KI = _DSA_OFF_V + _DSA_KV2
_DSA_OFF_WI = _DSA_OFF_KI + DSA_IDX_DIM
_DSA_COLS = _DSA_OFF_WI + LANES


def _dsa_proj_kernel(x_ref, gain_ref, shift_ref, scale_ref, w_ref, qgain_ref, kgain_ref,
                     cos64_ref, sin64_ref, cos128_ref, sin128_ref, g01_ref,
                     q_ref, g_ref, qi_ref, k_ref, v_ref, ki_ref, wi_ref):
    h = _prenorm(x_ref[0], gain_ref[...], scale_ref[0], shift_ref[0]).astype(BF16)
    g01 = g01_ref[...]
    cos64, sin64 = cos64_ref[...], sin64_ref[...]
    cos128, sin128 = cos128_ref[...], sin128_ref[...]
    lane = lax.broadcasted_iota(jnp.int32, cos64.shape, 1)
    lane_lo = (lane % DSA_HEAD_DIM) < DSA_HEAD_DIM // 2

    def head_norm_rope(raw, gain):
        ms = _group_sum(raw * raw, g01) * (1.0 / DSA_HEAD_DIM)
        return _rope64(raw * lax.rsqrt(ms + RMS_EPS) * gain, cos64, sin64, lane_lo)

    def col(off, n):
        return jnp.dot(h, w_ref[:, off:off + n], preferred_element_type=F32)

    tm = h.shape[0]
    row_lo64 = lax.broadcasted_iota(jnp.int32, (LANES, tm), 0) < DSA_HEAD_DIM
    q = head_norm_rope(col(_DSA_OFF_Q, _DSA_Q), qgain_ref[...]) * (LOG2E * DSA_HEAD_DIM ** -0.5)
    qt = q.T
    for hh in range(DSA_HEADS):
        blk = qt[(hh // 2) * LANES:(hh // 2 + 1) * LANES, :]
        keep = row_lo64 if hh % 2 == 0 else ~row_lo64
        q_ref[0, hh] = jnp.where(keep, blk, 0.0).astype(BF16)
    g_ref[0] = col(_DSA_OFF_G, _DSA_Q).astype(BF16)
    qit = _rope128(col(_DSA_OFF_QI, _DSA_QI), cos128, sin128).T.astype(BF16)
    for hh in range(DSA_IDX_HEADS):
        qi_ref[0, hh] = qit[hh * DSA_IDX_DIM:(hh + 1) * DSA_IDX_DIM, :]
    k2 = head_norm_rope(col(_DSA_OFF_K, _DSA_KV2), kgain_ref[...]).astype(BF16)
    for kv in range(DSA_KV_HEADS):
        k_ref[0, kv] = k2[:, kv * LANES:(kv + 1) * LANES]
    vt = col(_DSA_OFF_V, _DSA_KV2).T
    for kv in range(DSA_KV_HEADS):
        v_ref[0, kv] = jnp.where(row_lo64, vt[kv * LANES:(kv + 1) * LANES, :], 1.0).astype(BF16)
    ki_ref[0] = _rope128(col(_DSA_OFF_KI, DSA_IDX_DIM), cos128, sin128).astype(BF16)
    wit = (col(_DSA_OFF_WI, LANES) * DSA_IDX_SCALE).T
    wi_ref[0] = wit[0:DSA_IDX_HEADS, :]


def _dsa_weights(w_in):
    d = w_in.shape[0]
    q_end = _DSA_Q
    k_end = q_end + DSA_KV_HEADS * DSA_HEAD_DIM
    v_end = k_end + DSA_KV_HEADS * DSA_HEAD_DIM
    g_end = v_end + _DSA_Q
    qi_end = g_end + _DSA_QI
    wi_end = qi_end + DSA_IDX_HEADS
    wq, wk, wv, wg = w_in[:, :q_end], w_in[:, q_end:k_end], w_in[:, k_end:v_end], w_in[:, v_end:g_end]
    wqi, wwi, wki = w_in[:, g_end:qi_end], w_in[:, qi_end:wi_end], w_in[:, wi_end:]

    def dup(w):
        w = w.reshape(d, DSA_KV_HEADS, 1, DSA_HEAD_DIM)
        return jnp.broadcast_to(w, (d, DSA_KV_HEADS, 2, DSA_HEAD_DIM)).reshape(d, _DSA_KV2)

    wwi = jnp.pad(wwi, ((0, 0), (0, LANES - DSA_IDX_HEADS)))
    return jnp.concatenate([wq, wg, wqi, dup(wk), dup(wv), wki, wwi], axis=1).astype(BF16)


def _dsa_project(x, gain, mod3, w_in, q_gain, k_gain, tm=512):
    bsz, seq, d = x.shape
    tm = min(tm, seq)
    w = _dsa_weights(w_in)
    cos64, sin64 = _rope_tables(seq, DSA_HEAD_DIM, LANES // DSA_HEAD_DIM)
    cos128, sin128 = _rope_tables(seq, DSA_IDX_DIM, 1)
    shift_spec, scale_spec, _ = _mod_specs(d)
    table = pl.BlockSpec((tm, LANES), lambda b, t: (t, 0))
    row = lambda n, dt: jax.ShapeDtypeStruct((bsz, seq, n), dt)
    heads = lambda n: jax.ShapeDtypeStruct((bsz, n, seq, LANES), BF16)
    head_spec = lambda n: pl.BlockSpec((1, n, tm, LANES), lambda b, t: (b, 0, t, 0))
    heads_t = lambda n: jax.ShapeDtypeStruct((bsz, n, LANES, seq), BF16)
    head_t_spec = lambda n: pl.BlockSpec((1, n, LANES, tm), lambda b, t: (b, 0, 0, t))
    return pl.pallas_call(
        _dsa_proj_kernel,
        out_shape=[heads_t(DSA_HEADS), row(_DSA_Q, BF16), heads_t(DSA_IDX_HEADS),
                   heads(DSA_KV_HEADS), heads_t(DSA_KV_HEADS), row(DSA_IDX_DIM, BF16),
                   jax.ShapeDtypeStruct((bsz, DSA_IDX_HEADS, seq), F32)],
        grid=(bsz, seq // tm),
        in_specs=[pl.BlockSpec((1, tm, d), lambda b, t: (b, t, 0)),
                  _resident((1, d)), shift_spec, scale_spec, _resident(w.shape),
                  _resident((1, _DSA_Q)), _resident((1, _DSA_KV2)),
                  table, table, table, table, _resident((LANES, LANES))],
        out_specs=[head_t_spec(DSA_HEADS),
                   pl.BlockSpec((1, tm, _DSA_Q), lambda b, t: (b, t, 0)),
                   head_t_spec(DSA_IDX_HEADS),
                   head_spec(DSA_KV_HEADS),
                   head_t_spec(DSA_KV_HEADS),
                   pl.BlockSpec((1, tm, DSA_IDX_DIM), lambda b, t: (b, t, 0)),
                   pl.BlockSpec((1, DSA_IDX_HEADS, tm), lambda b, t: (b, 0, t))],
        compiler_params=_params("parallel", "parallel"),
        name="dsa_proj",
    )(x, gain.reshape(1, d), mod3, mod3, w,
      jnp.tile(q_gain, DSA_HEADS).reshape(1, _DSA_Q),
      jnp.tile(k_gain, _DSA_KV2 // DSA_HEAD_DIM).reshape(1, _DSA_KV2),
      cos64, sin64, cos128, sin128, _head_group_ones())


def _reduce_rows(x, op):
    part = op(x.reshape(x.shape[0] // 64, 64, x.shape[1]), axis=0)
    return op(part, axis=0, keepdims=True)


def _sortable_to_float(key):
    bits = jnp.where(key >= 0, key, key ^ jnp.int32(0x7FFFFFFF))
    return lax.bitcast_convert_type(bits, F32)


def _dsa_attn_kernel(n_sel, chunk, qi_ref, wi_ref, ki_ref, q_ref, k_ref, v_ref, tri_ref,
                     g_ref, x_ref, gate_ref, wout_ref, o_ref, score_ref, bias_ref, qk_ref):
    seq, qb = score_ref.shape
    per = chunk // qb
    needed = lax.div(pl.program_id(1) + per, per)
    for j in range(seq // chunk):
        width = chunk * (j + 1)
        block = _dsa_attn_block_all if width <= n_sel else _dsa_attn_block
        pl.when(needed == j + 1)(functools.partial(
            block, n_sel, width, qi_ref, wi_ref, ki_ref, q_ref, k_ref, v_ref,
            tri_ref, g_ref, x_ref, gate_ref, wout_ref, o_ref, score_ref, bias_ref, qk_ref))


def _dsa_attn_block(n_sel, width, qi_ref, wi_ref, ki_ref, q_ref, k_ref, v_ref, tri_ref,
                    g_ref, x_ref, gate_ref, wout_ref, o_ref, score_ref, bias_ref, qk_ref):
    blk = pl.program_id(1)
    qb = score_ref.shape[1]
    ki = ki_ref[0, 0:width, :]
    wi = wi_ref[0]
    stacked = DSA_HEADS // DSA_KV_HEADS

    def heads_of(ref, first):
        return jnp.concatenate([ref[0, first + j] for j in range(stacked)], axis=1)

    acc = None
    for grp in range(DSA_IDX_HEADS // stacked):
        logits = jnp.dot(ki, heads_of(qi_ref, grp * stacked), preferred_element_type=F32)
        for j in range(stacked):
            hh = grp * stacked + j
            term = wi[hh:hh + 1, :] * jnp.maximum(logits[:, j * qb:(j + 1) * qb], 0.0)
            acc = term if acc is None else acc + term
    q_pos = blk * qb + lax.broadcasted_iota(jnp.int32, (1, qb), 1)
    key_pos = lax.broadcasted_iota(jnp.int32, (width, 1), 0)
    score_ref[0:width, :] = jnp.where(key_pos <= q_pos, acc, -jnp.inf)

    k_sel = jnp.float32(n_sel)

    def count_ge(thr):
        return _reduce_rows(jnp.where(score_ref[0:width, :] >= thr, 1.0, 0.0), jnp.sum)

    def reaches(cand):
        return jnp.where(count_ge(_sortable_to_float(cand)) >= k_sel, 1, 0)

    def q_heads(kv):
        return jnp.concatenate([q_ref[0, kv * stacked + j] for j in range(stacked)], axis=1)

    half = width // 2
    per_step = 4

    def search_step(i, key):
        kv = i // 2
        rows = pl.ds(pl.multiple_of((i % 2) * half, half), half)
        qk_ref[kv, rows, :] = jnp.dot(k_ref[0, kv, rows, :], q_heads(kv),
                                      preferred_element_type=F32)
        for j in range(per_step):
            n = i * per_step + j
            cand = jnp.where(n == 0, jnp.int32(0),
                             key + lax.shift_left(jnp.int32(1), jnp.int32(31) - n))
            key = jnp.where(reaches(cand) > 0, cand, key)
        return key

    key0 = jnp.full((1, qb), -2 ** 31, jnp.int32)
    thr = _sortable_to_float(lax.fori_loop(0, 2 * DSA_KV_HEADS, search_step, key0))

    score = score_ref[0:width, :]
    gt = score > thr
    need = k_sel - _reduce_rows(jnp.where(gt, 1.0, 0.0), jnp.sum)
    take_all = q_pos < n_sel
    tri = tri_ref[...]
    run = jnp.zeros((1, qb), F32)
    for c in range(width // LANES):
        sl = slice(c * LANES, (c + 1) * LANES)
        eq = jnp.where(score[sl, :] == thr, 1.0, 0.0)
        incl = jnp.dot(tri, eq.astype(BF16), preferred_element_type=F32)
        tie = jnp.where((incl - eq + run) < need, eq, 0.0)
        run = run + incl[LANES - 1:LANES, :]
        chosen = jnp.where(take_all, 1.0, jnp.where(gt[sl, :], 1.0, tie))
        causal = key_pos[sl, :] <= q_pos
        bias_ref[sl, :] = jnp.where(causal, (1.0 - chosen) * NEG_BIG, NEG_BIG)

    _dsa_attn_tail(width, v_ref, g_ref, x_ref, gate_ref, wout_ref, o_ref, bias_ref, qk_ref)


def _dsa_attn_block_all(n_sel, width, qi_ref, wi_ref, ki_ref, q_ref, k_ref, v_ref, tri_ref,
                        g_ref, x_ref, gate_ref, wout_ref, o_ref, score_ref, bias_ref, qk_ref):
    qb = bias_ref.shape[1]
    stacked = DSA_HEADS // DSA_KV_HEADS
    q_pos = pl.program_id(1) * qb + lax.broadcasted_iota(jnp.int32, (1, qb), 1)
    key_pos = lax.broadcasted_iota(jnp.int32, (width, 1), 0)
    bias_ref[0:width, :] = jnp.where(key_pos <= q_pos, 0.0, NEG_BIG)
    for kv in range(DSA_KV_HEADS):
        q4 = jnp.concatenate([q_ref[0, kv * stacked + j] for j in range(stacked)], axis=1)
        qk_ref[kv, 0:width, :] = jnp.dot(k_ref[0, kv, 0:width, :], q4, preferred_element_type=F32)
    _dsa_attn_tail(width, v_ref, g_ref, x_ref, gate_ref, wout_ref, o_ref, bias_ref, qk_ref)


def _dsa_attn_tail(width, v_ref, g_ref, x_ref, gate_ref, wout_ref, o_ref, bias_ref, qk_ref):
    qb = bias_ref.shape[1]
    stacked = DSA_HEADS // DSA_KV_HEADS
    bias = bias_ref[0:width, :]
    outs = []
    for kv in range(DSA_KV_HEADS):
        s4 = qk_ref[kv, 0:width, :]
        ps = []
        for j in range(stacked):
            s = s4[:, j * qb:(j + 1) * qb] + bias
            ps.append(jnp.exp2(s - _reduce_rows(s, jnp.max)).astype(BF16))
        ov = jnp.dot(v_ref[0, kv, :, 0:width], jnp.concatenate(ps, axis=1),
                     preferred_element_type=F32)
        ov = ov[0:DSA_HEAD_DIM, :] / ov[DSA_HEAD_DIM:DSA_HEAD_DIM + 1, :]
        for c in range(stacked // 2):
            even = ov[:, (2 * c) * qb:(2 * c + 1) * qb]
            odd = ov[:, (2 * c + 1) * qb:(2 * c + 2) * qb]
            outs.append(jnp.concatenate([even, odd], axis=0).T)
    o_ref[0] = _gated_residual(jnp.concatenate(outs, axis=1), g_ref[0].astype(F32), x_ref[0],
                               gate_ref[0], wout_ref[...])


def _dsa_attention(q, qi, k2, vt, ki, wi, g, x, mod3, w_out):
    bsz, seq, d = x.shape
    qb = DSA_QBLOCK
    n_sel = min(DSA_TOPK, seq // 4)
    chunk = min(DSA_KEY_CHUNK, seq // 2)
    return pl.pallas_call(
        functools.partial(_dsa_attn_kernel, n_sel, chunk),
        out_shape=jax.ShapeDtypeStruct((bsz, seq, d), F32),
        grid=(bsz, seq // qb),
        in_specs=[pl.BlockSpec((1, DSA_IDX_HEADS, LANES, qb), lambda b, i: (b, 0, 0, i)),
                  pl.BlockSpec((1, DSA_IDX_HEADS, qb), lambda b, i: (b, 0, i)),
                  pl.BlockSpec((1, seq, DSA_IDX_DIM), lambda b, i: (b, 0, 0)),
                  pl.BlockSpec((1, DSA_HEADS, LANES, qb), lambda b, i: (b, 0, 0, i)),
                  pl.BlockSpec((1, DSA_KV_HEADS, seq, LANES), lambda b, i: (b, 0, 0, 0)),
                  pl.BlockSpec((1, DSA_KV_HEADS, LANES, seq), lambda b, i: (b, 0, 0, 0)),
                  _resident((LANES, LANES)),
                  pl.BlockSpec((1, qb, _DSA_Q), lambda b, i: (b, i, 0)),
                  pl.BlockSpec((1, qb, d), lambda b, i: (b, i, 0)),
                  _mod_specs(d)[2],
                  _resident((_DSA_Q, d))],
        out_specs=pl.BlockSpec((1, qb, d), lambda b, i: (b, i, 0)),
        scratch_shapes=[pltpu.VMEM((seq, qb), F32), pltpu.VMEM((seq, qb), F32),
                        pltpu.VMEM((DSA_KV_HEADS, seq, (DSA_HEADS // DSA_KV_HEADS) * qb), F32)],
        compiler_params=_params("parallel", "parallel"),
        name="dsa_attn",
    )(qi, wi, ki, q, k2, vt, _lower_tri_ones(LANES), g, x, mod3, w_out.astype(BF16))


def _dsa_layer(x, gain, mod3, w_in, q_gain, k_gain, w_out):
    q, g, qi, k2, vt, ki, wi = _dsa_project(x, gain, mod3, w_in, q_gain, k_gain)
    return _dsa_attention(q, qi, k2, vt, ki, wi, g, x, mod3, w_out)


def _lru_kernel(x_ref, gain_ref, shift_ref, scale_ref, gate_ref, win_ref, cw_ref, cb_ref,
                wa_ref, ba_ref, wx_ref, bx_ref, lam_ref, wout_ref, o_ref,
                h_buf, ubuf, a_buf, b_buf, hs_buf, g_buf, h_carry):
    tt = x_ref.shape[1]
    width = ubuf.shape[1]
    halo = SUBLANES

    @pl.when(pl.program_id(1) == 0)
    def _():
        ubuf[0:halo, :] = jnp.zeros((halo, width), F32)
        h_carry[...] = jnp.zeros_like(h_carry)

    h_buf[...] = _prenorm(x_ref[0], gain_ref[...], scale_ref[0], shift_ref[0]).astype(BF16)
    ubuf[halo:halo + tt, :] = jnp.dot(h_buf[...], win_ref[:, 0:width], preferred_element_type=F32)
    cw = cw_ref[...]
    u = cb_ref[...]
    for j in range(LRU_CONV):
        start = halo - (LRU_CONV - 1) + j
        u = u + cw[j:j + 1, :] * ubuf[start:start + tt, :]
    ubuf[0:halo, :] = ubuf[tt:tt + halo, :]

    g_buf[...] = jnp.dot(h_buf[...], win_ref[:, width:2 * width], preferred_element_type=F32)

    sp = _softplus(-lam_ref[...])
    for c in range(width // LRU_GROUP):
        sl = slice(c * LRU_GROUP, (c + 1) * LRU_GROUP)
        uc = u[:, sl]
        ub = uc.astype(BF16)
        r = jax.nn.sigmoid(jnp.dot(ub, wa_ref[c], preferred_element_type=F32) + ba_ref[:, sl])
        i = jax.nn.sigmoid(jnp.dot(ub, wx_ref[c], preferred_element_type=F32) + bx_ref[:, sl])
        a = jnp.exp(-LRU_C * r * sp[:, sl])
        a_buf[:, sl] = a
        b_buf[:, sl] = jnp.sqrt(1.0 - a * a) * (i * uc)

    row = lax.broadcasted_iota(jnp.int32, (SUBLANES, width), 0)

    def group(gi, h_prev):
        r0 = pl.multiple_of(gi * SUBLANES, SUBLANES)
        a = a_buf[pl.ds(r0, SUBLANES), :]
        b = b_buf[pl.ds(r0, SUBLANES), :]
        for s in (1, 2, 4):
            ok = row >= s
            b = jnp.where(ok, a * pltpu.roll(b, s, 0) + b, b)
            a = jnp.where(ok, a * pltpu.roll(a, s, 0), a)
        h = a * h_prev + b
        hs_buf[pl.ds(r0, SUBLANES), :] = h
        return jnp.broadcast_to(h[SUBLANES - 1:SUBLANES, :], (SUBLANES, width))

    h_carry[...] = lax.fori_loop(0, tt // SUBLANES, group, h_carry[...])
    o_ref[0] = _gated_residual(hs_buf[...], g_buf[...], x_ref[0], gate_ref[0], wout_ref[...])


def _lru_gate_blocks(w):
    nb, bd, _ = w.shape
    per = LRU_GROUP // bd
    w = w.reshape(nb // per, per, bd, bd)
    eye = jnp.eye(per, dtype=w.dtype)
    return jnp.einsum('gpcd,pq->gpcqd', w, eye).reshape(nb // per, LRU_GROUP, LRU_GROUP).astype(BF16)


def _lru_layer(x, gain, mod3, w_in, conv_w, conv_b, gate_a_w, gate_a_b, gate_x_w, gate_x_b, lam,
               w_out, tt=512):
    bsz, seq, d = x.shape
    tt = min(tt, seq)
    width = w_in.shape[1] // 2
    ngroups = width // LRU_GROUP
    tile = pl.BlockSpec((1, tt, d), lambda b, t: (b, t, 0))
    vec = lambda: _resident((1, width))
    gatew = lambda: _resident((ngroups, LRU_GROUP, LRU_GROUP))
    cw = jnp.pad(conv_w, ((0, SUBLANES - LRU_CONV), (0, 0)))
    return pl.pallas_call(
        _lru_kernel,
        out_shape=jax.ShapeDtypeStruct((bsz, seq, d), F32),
        grid=(bsz, seq // tt),
        in_specs=[tile, _resident((1, d)), *_mod_specs(d), _resident((d, 2 * width)),
                  _resident((SUBLANES, width)), vec(), gatew(), vec(), gatew(), vec(), vec(),
                  _resident((width, d))],
        out_specs=tile,
        scratch_shapes=[pltpu.VMEM((tt, d), BF16),
                        pltpu.VMEM((tt + SUBLANES, width), F32),
                        pltpu.VMEM((tt, width), F32), pltpu.VMEM((tt, width), F32),
                        pltpu.VMEM((tt, width), F32), pltpu.VMEM((tt, width), F32),
                        pltpu.VMEM((SUBLANES, width), F32)],
        compiler_params=_params("parallel", "arbitrary"),
        name="rglru_layer",
    )(x, gain.reshape(1, d), mod3, mod3, mod3, w_in.astype(BF16), cw, conv_b.reshape(1, width),
      _lru_gate_blocks(gate_a_w), gate_a_b.reshape(1, width),
      _lru_gate_blocks(gate_x_w), gate_x_b.reshape(1, width), lam.reshape(1, width),
      w_out.astype(BF16))


def _rwkv_kernel(x_ref, gain_ref, shift_ref, scale_ref, gate_ref, mu_ref, w_ref, w1_ref, w2_ref,
                 a1_ref, a2_ref, w0_ref, a0_ref, kkw_ref, ka_ref, rk_ref, lnw_ref, lnb_ref,
                 ltri_ref, wout_ref, o_ref,
                 r_s, lw_s, k_s, v_s, kk_s, a_s, bonus_s, g_s, y_s, carry, state):
    tm = x_ref.shape[1]
    d = x_ref.shape[2]
    cs = RWKV_CHUNK
    hd = RWKV_HEAD_DIM

    @pl.when(pl.program_id(1) == 0)
    def _():
        carry[...] = jnp.zeros_like(carry)
        state[...] = jnp.zeros_like(state)

    lo_t = lax.broadcasted_iota(jnp.int32, (tm, LANES), 1) < hd
    h = _prenorm(x_ref[0], gain_ref[...], scale_ref[0], shift_ref[0])
    first = lax.broadcasted_iota(jnp.int32, (tm, d), 0) == 0
    h_prev = jnp.where(first, carry[0:1, :], pltpu.roll(h, 1, 0))
    carry[...] = jnp.broadcast_to(h[tm - 1:tm, :], carry.shape)
    delta = h_prev - h
    mu = mu_ref[...]

    def mix(n):
        return (h + delta * mu[n:n + 1, :]).astype(BF16)

    r = jnp.dot(mix(0), w_ref[0], preferred_element_type=F32)
    k = jnp.dot(mix(1), w_ref[1], preferred_element_type=F32)
    v = jnp.dot(mix(2), w_ref[2], preferred_element_type=F32)
    g_s[...] = jnp.dot(mix(3), w_ref[3], preferred_element_type=F32)
    w_lora = _dot(jnp.tanh(jnp.dot(mix(4), w1_ref[...], preferred_element_type=F32)), w2_ref[...])
    w_log = -_softplus(-(w0_ref[...] + w_lora)) - 0.5
    lw_s[...] = -jnp.exp(w_log)
    a_lora = _dot(jnp.dot(mix(5), a1_ref[...], preferred_element_type=F32), a2_ref[...])
    a = jax.nn.sigmoid(a0_ref[...] + a_lora)
    kk = k * kkw_ref[...]
    kk = kk / jnp.maximum(jnp.sqrt(_group_sum_lanes(kk * kk, lo_t)), 1e-12)
    k = k * (1.0 + (a - 1.0) * ka_ref[...])
    r_s[...] = r
    k_s[...] = k
    v_s[...] = v
    kk_s[...] = kk
    a_s[...] = a
    bonus_s[...] = _group_sum_lanes(r * k * rk_ref[...], lo_t) * v

    lane = lax.broadcasted_iota(jnp.int32, (cs, LANES), 1)
    lo = lane < hd
    ri = lax.broadcasted_iota(jnp.int32, (LANES, LANES), 0)
    ci = lax.broadcasted_iota(jnp.int32, (LANES, LANES), 1)
    same = (ri < cs) == (ci < cs)
    strict = same & (ci < ri)
    incl = same & (ci <= ri)
    eye = jnp.where(ri == ci, 1.0, 0.0)
    ltri = ltri_ref[...]

    def stack(z):
        return jnp.concatenate([z, z], axis=0)

    def split_heads(z):
        return jnp.concatenate([jnp.where(lo, z, 0.0), jnp.where(lo, 0.0, z)], axis=0)

    def own(z):
        return jnp.where(lo, z[0:cs], z[cs:2 * cs])

    pairs = range(d // LANES)
    lanes = [slice(p * LANES, (p + 1) * LANES) for p in pairs]

    def chunk(c, _):
        rows = pl.ds(pl.multiple_of(c * cs, cs), cs)
        lw = lw_s[rows, :]
        g_inc = _dot_exact_lhs(ltri, lw)
        g_last = g_inc[cs - 1:cs, :]
        kk = kk_s[rows, :]
        kc = k_s[rows, :]
        vc = v_s[rows, :]
        bvec = kk * a_s[rows, :]
        e_neg = jnp.exp(-g_inc)
        e_end = jnp.exp(g_last - g_inc)
        at = -kk * jnp.exp(g_inc - lw)
        rt = r_s[rows, :] * jnp.exp(g_inc)
        bt = bvec * e_neg
        kt = kc * e_neg
        bend = (bvec * e_end).astype(BF16)
        kend = (kc * e_end).astype(BF16)
        decay = jnp.exp(g_last)

        lhs = [jnp.concatenate([at[:, s], rt[:, s]], axis=0).astype(BF16) for s in lanes]
        nbk = [_dot_nt(lhs[p], jnp.concatenate([split_heads(bt[:, lanes[p]]),
                                                split_heads(kt[:, lanes[p]])], axis=0))
               for p in pairs]
        n_ab = [jnp.where(strict, stack(nbk[p][0:cs, 0:LANES]), 0.0) for p in pairs]
        n_rb = [jnp.where(incl, stack(nbk[p][cs:2 * cs, 0:LANES]), 0.0).astype(BF16) for p in pairs]
        n_ak = [jnp.where(strict, stack(nbk[p][0:cs, LANES:2 * LANES]), 0.0) for p in pairs]
        n_rk = [jnp.where(incl, stack(nbk[p][cs:2 * cs, LANES:2 * LANES]), 0.0) for p in pairs]

        inv = [eye + n_ab[p] for p in pairs]
        pw = [_dot(n_ab[p], n_ab[p]) for p in pairs]
        for _stage in range(4):
            res = [_dot(pw[p], jnp.concatenate([pw[p], inv[p]], axis=1)) for p in pairs]
            pw = [res[p][:, 0:LANES] for p in pairs]
            inv = [inv[p] + res[p][:, LANES:2 * LANES] for p in pairs]
        inv = [(inv[p] + _dot(pw[p], inv[p])).astype(BF16) for p in pairs]

        vv = [stack(vc[:, s]).astype(BF16) for s in lanes]
        nv = [_dot(jnp.concatenate([n_ak[p], n_rk[p]], axis=0), vv[p]) for p in pairs]
        w1 = [own(nv[p][0:LANES]) for p in pairs]
        tz = [_dot(inv[p], jnp.concatenate([stack(at[:, lanes[p]]), stack(w1[p])], axis=1))
              for p in pairs]
        a2 = [own(tz[p][:, 0:LANES]) for p in pairs]
        u0 = [own(tz[p][:, LANES:2 * LANES]) for p in pairs]
        rz = [_dot(n_rb[p], jnp.concatenate([stack(a2[p]), stack(u0[p])], axis=1)) for p in pairs]
        r2 = [rt[:, lanes[p]] + own(rz[p][:, 0:LANES]) for p in pairs]
        y0 = [own(rz[p][:, LANES:2 * LANES]) + own(nv[p][LANES:2 * LANES]) for p in pairs]
        mlr = [jnp.where(same, _dot(a2[p].T, bend[:, lanes[p]]), 0.0) for p in pairs]
        c0 = [jnp.where(same, _dot(jnp.concatenate([u0[p], vc[:, lanes[p]]], axis=0).T,
                                   jnp.concatenate([bend[:, lanes[p]], kend[:, lanes[p]]], axis=0)), 0.0)
              for p in pairs]

        ys = []
        for p in pairs:
            s_bd = state[p]
            ys.append(_dot_nt(r2[p], s_bd) + y0[p])
            state[p] = s_bd * decay[:, lanes[p]] + _dot(s_bd, mlr[p]) + c0[p]
        y = jnp.concatenate(ys, axis=1)
        mean = _group_sum_lanes(y, lo) * (1.0 / hd)
        yc = y - mean
        var = _group_sum_lanes(yc * yc, lo) * (1.0 / hd)
        y_s[rows, :] = (yc * lax.rsqrt(var + RWKV_GN_EPS) * lnw_ref[...] + lnb_ref[...]
                        + bonus_s[rows, :])
        return 0

    lax.fori_loop(0, tm // cs, chunk, 0, unroll=2)
    o_ref[0] = _gated_residual(y_s[...], g_s[...], x_ref[0], gate_ref[0], wout_ref[...])


def _rwkv_layer(x, gain, mod3, mu, w_in, w0, w1, w2, a0, a1, a2, k_k, k_a, r_k, ln_w, ln_b, w_out,
                tm=512):
    bsz, seq, d = x.shape
    tm = min(tm, seq)
    rank = w1.shape[1]
    pad_c = lambda w: jnp.pad(w, ((0, 0), (0, LANES - rank))).astype(BF16)
    pad_r = lambda w: jnp.pad(w, ((0, LANES - rank), (0, 0))).astype(BF16)
    vec = lambda p: p.reshape(1, d)
    tile = pl.BlockSpec((1, tm, d), lambda b, t: (b, t, 0))
    buf = pltpu.VMEM((tm, d), F32)
    return pl.pallas_call(
        _rwkv_kernel,
        out_shape=jax.ShapeDtypeStruct((bsz, seq, d), F32),
        grid=(bsz, seq // tm),
        in_specs=[tile, _resident((1, d)), *_mod_specs(d), _resident((SUBLANES, d)),
                  _resident((4, d, d)), _resident((d, LANES)), _resident((LANES, d)),
                  _resident((d, LANES)), _resident((LANES, d))]
                 + [_resident((1, d))] * 7
                 + [_resident((RWKV_CHUNK, RWKV_CHUNK)), _resident((d, d))],
        out_specs=tile,
        scratch_shapes=[buf] * 9 + [pltpu.VMEM((SUBLANES, d), F32),
                                    pltpu.VMEM((d // LANES, LANES, LANES), F32)],
        compiler_params=_params("parallel", "arbitrary"),
        name="rwkv_layer",
    )(x, gain.reshape(1, d), mod3, mod3, mod3,
      jnp.pad(mu, ((0, SUBLANES - mu.shape[0]), (0, 0))),
      w_in.astype(BF16), pad_c(w1), pad_r(w2), pad_c(a1), pad_r(a2), vec(w0), vec(a0),
      vec(k_k), vec(k_a), vec(r_k), vec(ln_w), vec(ln_b),
      _lower_tri_ones(RWKV_CHUNK), w_out.astype(BF16))


def _gla_kernel(x_ref, gain_ref, shift_ref, scale_ref, gate_ref, win_ref, w2_ref, ab_ref,
                ngain_ref, ltri_ref, wout_ref, o_ref, o_buf, state):
    tt = x_ref.shape[1]
    key_dim = w2_ref.shape[1]
    val_dim = wout_ref.shape[0]
    dk = key_dim // GLA_HEADS
    dv = val_dim // GLA_HEADS
    cs = GLA_CHUNK

    @pl.when(pl.program_id(1) == 0)
    def _():
        state[...] = jnp.zeros_like(state)

    h = _prenorm(x_ref[0], gain_ref[...], scale_ref[0], shift_ref[0]).astype(BF16)

    def col(off, n):
        return jnp.dot(h, win_ref[:, off:off + n], preferred_element_type=F32)

    q = col(0, key_dim)
    k = col(key_dim, key_dim)
    v = col(2 * key_dim, val_dim)
    g = col(2 * key_dim + val_dim, val_dim)
    a_low = col(2 * key_dim + 2 * val_dim, LANES)

    ri = lax.broadcasted_iota(jnp.int32, (cs, cs), 0)
    ci = lax.broadcasted_iota(jnp.int32, (cs, cs), 1)
    causal = ci <= ri
    ltri = ltri_ref[...]
    z = _dot(a_low, w2_ref[...]) + ab_ref[...]
    log_alpha = -_softplus(-z) * (1.0 / GLA_GATE_NORM)

    chunks = range(tt // cs)
    heads = range(GLA_HEADS)
    rows = [slice(c * cs, (c + 1) * cs) for c in chunks]
    ks = [slice(hh * dk, (hh + 1) * dk) for hh in heads]
    vs = [slice(hh * dv, (hh + 1) * dv) for hh in heads]
    q_dec, k_inv, k_end, decay = [], [], [], []
    for c in chunks:
        cum = _dot_exact_lhs(ltri, log_alpha[rows[c], :])
        last = cum[cs - 1:cs, :]
        kc = k[rows[c], :]
        q_dec.append((q[rows[c], :] * (dk ** -0.5) * jnp.exp(cum)).astype(BF16))
        k_inv.append(kc * jnp.exp(-cum))
        k_end.append(kc * jnp.exp(last - cum))
        decay.append(jnp.exp(last))
    att = [[jnp.where(causal, _dot_nt(q_dec[c][:, ks[hh]], k_inv[c][:, ks[hh]]), 0.0)
            for hh in heads] for c in chunks]
    o_intra = [[_dot(att[c][hh], v[rows[c], vs[hh]]) for hh in heads] for c in chunks]
    upd = [[_dot(v[rows[c], vs[hh]].T, k_end[c][:, ks[hh]]) for hh in heads] for c in chunks]
    start = [[None] * GLA_HEADS for _ in chunks]
    for hh in heads:
        st = state[hh]
        for c in chunks:
            start[c][hh] = st
            st = st * decay[c][:, ks[hh]] + upd[c][hh]
        state[hh] = st
    for c in chunks:
        for hh in heads:
            o = o_intra[c][hh] + _dot_nt(q_dec[c][:, ks[hh]], start[c][hh])
            ms = jnp.mean(o * o, axis=-1, keepdims=True)
            o_buf[rows[c], vs[hh]] = o * lax.rsqrt(ms + RMS_EPS) * ngain_ref[...]

    o_ref[0] = _gated_residual(o_buf[...], g, x_ref[0], gate_ref[0], wout_ref[...])


def _gla_layer(x, gain, mod3, w_in, alpha_w2, alpha_b, norm_gain, w_out, tt=512):
    bsz, seq, d = x.shape
    tt = min(tt, seq)
    key_dim, val_dim = alpha_w2.shape[1], w_out.shape[0]
    dk, dv = key_dim // GLA_HEADS, val_dim // GLA_HEADS
    w = jnp.pad(w_in, ((0, 0), (0, LANES - GLA_GATE_RANK))).astype(BF16)
    w2 = jnp.pad(alpha_w2, ((0, LANES - GLA_GATE_RANK), (0, 0))).astype(BF16)
    tile = pl.BlockSpec((1, tt, d), lambda b, t: (b, t, 0))
    return pl.pallas_call(
        _gla_kernel,
        out_shape=jax.ShapeDtypeStruct((bsz, seq, d), F32),
        grid=(bsz, seq // tt),
        in_specs=[tile, _resident((1, d)), *_mod_specs(d), _resident(w.shape),
                  _resident((LANES, key_dim)), _resident((1, key_dim)), _resident((1, dv)),
                  _resident((GLA_CHUNK, GLA_CHUNK)), _resident((val_dim, d))],
        out_specs=tile,
        scratch_shapes=[pltpu.VMEM((tt, val_dim), F32), pltpu.VMEM((GLA_HEADS, dv, dk), F32)],
        compiler_params=_params("parallel", "arbitrary"),
        name="gla_layer",
    )(x, gain.reshape(1, d), mod3, mod3, mod3, w, w2, alpha_b.reshape(1, key_dim),
      norm_gain.reshape(1, dv), _lower_tri_ones(GLA_CHUNK), w_out.astype(BF16))


def kernel(x, c, ln_gain, mod_w, mod_b, dsa_w_in, dsa_q_gain, dsa_k_gain, dsa_w_out, lru_w_in, lru_conv_w, lru_conv_b, lru_gate_a_w, lru_gate_a_b, lru_gate_x_w, lru_gate_x_b, lru_lambda, lru_w_out, rwkv_mu, rwkv_w_in, rwkv_w0, rwkv_w1, rwkv_w2, rwkv_a0, rwkv_a1, rwkv_a2, rwkv_k_k, rwkv_k_a, rwkv_r_k, rwkv_ln_w, rwkv_ln_b, rwkv_w_out, gla_w_in, gla_alpha_w2, gla_alpha_b, gla_norm_gain, gla_w_out):
    depth = mod_w.shape[0]
    bsz, _, d = x.shape
    mod = _modulation(c, mod_w, mod_b)
    for layer in range(depth):
        mixer, r = layer % 4, layer // 4
        mod3 = mod[layer].reshape(bsz, 1, 3 * d)
        gain = ln_gain[layer]
        if mixer == 0:
            x = _dsa_layer(x, gain, mod3, dsa_w_in[r], dsa_q_gain[r], dsa_k_gain[r], dsa_w_out[r])
        elif mixer == 1:
            x = _lru_layer(x, gain, mod3, lru_w_in[r], lru_conv_w[r], lru_conv_b[r], lru_gate_a_w[r],
                           lru_gate_a_b[r], lru_gate_x_w[r], lru_gate_x_b[r], lru_lambda[r], lru_w_out[r])
        elif mixer == 2:
            x = _rwkv_layer(x, gain, mod3, rwkv_mu[r], rwkv_w_in[r], rwkv_w0[r], rwkv_w1[r], rwkv_w2[r],
                            rwkv_a0[r], rwkv_a1[r], rwkv_a2[r], rwkv_k_k[r], rwkv_k_a[r],
                            rwkv_r_k[r].reshape(-1), rwkv_ln_w[r], rwkv_ln_b[r], rwkv_w_out[r])
        else:
            x = _gla_layer(x, gain, mod3, gla_w_in[r], gla_alpha_w2[r], gla_alpha_b[r],
                           gla_norm_gain[r], gla_w_out[r])
    return x
```
